```python
import jax, jax.numpy as jnp
from jax import lax
import numpy as np

D_MODEL = 4096
BATCH = 2
SEQ = 4096
DEPTH = 2

CHUNK = 64
MIX_WIDTH = D_MODEL
HALF = MIX_WIDTH // 2
EPS = 1e-6
CONV_A_WIDTH = 3
SGU_BLOCK = 128
SGU_GROUPS = 8
POOL_WINDOWS = (2, 4, 8, 16)
POOL_GROUP = HALF // len(POOL_WINDOWS)
GDN_HEADS = 16
GDN_DK = HALF // GDN_HEADS
GDN_DV = HALF // GDN_HEADS
GDN_CONV = 4
PEER_HEADS = 8
PEER_KEYS = 128
PEER_EXPERTS = PEER_KEYS * PEER_KEYS
PEER_QDIM = 256
PEER_TOPK = 16
PEER_BLOCK = 128

N_EVEN = (DEPTH + 1) // 2
N_ODD = DEPTH // 2
AB_IN = 3 * HALF + 2 * HALF
CD_IN = HALF + 4 * HALF + 2 * GDN_HEADS

kernel_name = "hybrid_conv_sgu_pool_deltanet_peer"

F32 = jnp.float32


def rmsnorm(x, g):
    xf = x.astype(F32)
    y = xf * lax.rsqrt(jnp.mean(xf * xf, axis=-1, keepdims=True) + EPS)
    return (y * g.astype(F32)).astype(x.dtype)


def causal_dwconv(x, w):
    k = w.shape[1]
    rhs = jnp.transpose(w)[:, None, :].astype(x.dtype)
    return lax.conv_general_dilated(x, rhs, window_strides=(1,), padding=[(k - 1, 0)],
                                    dimension_numbers=('NWC', 'WIO', 'NWC'),
                                    feature_group_count=x.shape[-1])


def l2norm(x):
    return x * lax.rsqrt(jnp.sum(x * x, axis=-1, keepdims=True) + EPS)


def short_conv_mixer(gate_b, gate_c, h, conv_w):
    return gate_b * causal_dwconv(gate_c * h, conv_w)


def spatial_gating_mixer(uv, ln_g, ln_b, w_s, b_s):
    u, v = jnp.split(jax.nn.gelu(uv), 2, axis=-1)
    vf = v.astype(F32)
    mu = jnp.mean(vf, axis=-1, keepdims=True)
    var = jnp.mean(jnp.square(vf - mu), axis=-1, keepdims=True)
    v = ((vf - mu) * lax.rsqrt(var + EPS) * ln_g.astype(F32) + ln_b.astype(F32)).astype(uv.dtype)
    b, s, c = v.shape
    vb = v.reshape(b, s // SGU_BLOCK, SGU_BLOCK, SGU_GROUPS, c // SGU_GROUPS)
    lower = jnp.tril(jnp.ones((SGU_BLOCK, SGU_BLOCK), dtype=bool))
    w = jnp.where(lower, w_s, 0.0).astype(v.dtype)
    mixed = jnp.einsum('gij,bnjgc->bnigc', w, vb) + jnp.transpose(b_s).astype(v.dtype)[None, None, :, :, None]
    return u * mixed.reshape(b, s, c)


def multiscale_pool_mixer(x, w_grp, scale):
    b, s, c = x.shape
    xf = x.reshape(b, s, len(POOL_WINDOWS), POOL_GROUP).astype(F32)
    cs = jnp.cumsum(xf, axis=1)
    t = jnp.arange(s)
    outs = []
    for i, win in enumerate(POOL_WINDOWS):
        c_i = cs[:, :, i]
        shifted = jnp.pad(c_i, ((0, 0), (win, 0), (0, 0)))[:, :s]
        cnt = jnp.minimum(t + 1, win).astype(F32)[None, :, None]
        outs.append((c_i - shifted) / cnt - xf[:, :, i])
    pooled = jnp.stack(outs, axis=2).astype(x.dtype)
    y = jnp.einsum('bsgi,gio->bsgo', pooled, w_grp.astype(x.dtype))
    return y.reshape(b, s, c) * scale


def gated_delta_rule(q, k, v, g, beta):
    b, s, h, dk = q.shape
    dv = v.shape[-1]
    n = s // CHUNK

    def chunks(t):
        return jnp.moveaxis(t.reshape(b, n, CHUNK, h, *t.shape[3:]), 3, 1)

    qc, kc, vc, bc = chunks(q), chunks(k), chunks(v), chunks(beta)
    gc = jnp.cumsum(chunks(g), axis=-1)
    lower = jnp.tril(jnp.ones((CHUNK, CHUNK), dtype=bool))
    strict = jnp.tril(jnp.ones((CHUNK, CHUNK), dtype=bool), -1)
    diff = gc[..., :, None] - gc[..., None, :]
    decay = jnp.where(lower, jnp.exp(jnp.where(lower, diff, 0.0)), 0.0)
    kb = kc * bc[..., None]
    m = jnp.where(strict, jnp.einsum('bhnid,bhnjd->bhnij', kb, kc) * decay, 0.0)
    eye = jnp.eye(CHUNK, dtype=F32)
    t_inv = lax.linalg.triangular_solve(eye + m, jnp.broadcast_to(eye, m.shape),
                                        left_side=True, lower=True, unit_diagonal=True)
    u_val = t_inv @ (vc * bc[..., None])
    w_key = t_inv @ (kb * jnp.exp(gc)[..., None])
    attn = jnp.einsum('bhnid,bhnjd->bhnij', qc, kc) * decay
    q_dec = qc * jnp.exp(gc)[..., None]
    g_last = gc[..., -1]
    k_dec = kc * jnp.exp(g_last[..., None] - gc)[..., None]

    def step(state, xs):
        u_i, w_i, a_i, qd_i, kd_i, gl_i = xs
        v_new = u_i - w_i @ state
        o = qd_i @ state + a_i @ v_new
        state = state * jnp.exp(gl_i)[..., None, None] + jnp.swapaxes(kd_i, -1, -2) @ v_new
        return state, o

    xs = tuple(jnp.moveaxis(t, 2, 0) for t in (u_val, w_key, attn, q_dec, k_dec, g_last))
    _, o = lax.scan(step, jnp.zeros((b, h, dk, dv), F32), xs)
    return jnp.transpose(o, (1, 0, 3, 2, 4)).reshape(b, s, h, dv)


def gated_deltanet_mixer(qkv, a, bgate, z, conv_w, a_log, dt_bias, o_gain):
    dt = qkv.dtype
    b, s, _ = qkv.shape
    qkv = jax.nn.silu(causal_dwconv(qkv, conv_w)).astype(F32)
    q, k, v = jnp.split(qkv, 3, axis=-1)
    q = l2norm(q.reshape(b, s, GDN_HEADS, GDN_DK)) * (GDN_DK ** -0.5)
    k = l2norm(k.reshape(b, s, GDN_HEADS, GDN_DK))
    v = v.reshape(b, s, GDN_HEADS, GDN_DV)
    g = -jnp.exp(a_log.astype(F32)) * jax.nn.softplus(a.astype(F32) + dt_bias.astype(F32))
    beta = jax.nn.sigmoid(bgate.astype(F32))
    o = gated_delta_rule(q, k, v, g, beta)
    o = o * lax.rsqrt(jnp.mean(o * o, axis=-1, keepdims=True) + EPS) * o_gain.astype(F32)
    o = o * jax.nn.silu(z.astype(F32).reshape(b, s, GDN_HEADS, GDN_DV))
    return o.reshape(b, s, HALF).astype(dt)


def mixer_ab(x, w_in, conv_a, ln_g, ln_b, w_s, b_s, w_out):
    p = x @ w_in
    ya = short_conv_mixer(p[..., :HALF], p[..., HALF:2 * HALF], p[..., 2 * HALF:3 * HALF], conv_a)
    yb = spatial_gating_mixer(p[..., 3 * HALF:], ln_g, ln_b, w_s, b_s)
    return jnp.concatenate([ya, yb], axis=-1) @ w_out


def mixer_cd(x, w_in, pool_w, pool_scale, conv_qkv, a_log, dt_bias, o_gain, w_out):
    p = x @ w_in
    yc = multiscale_pool_mixer(p[..., :HALF], pool_w, pool_scale)
    yd = gated_deltanet_mixer(p[..., HALF:4 * HALF], p[..., 5 * HALF:5 * HALF + GDN_HEADS],
                              p[..., 5 * HALF + GDN_HEADS:], p[..., 4 * HALF:5 * HALF],
                              conv_qkv, a_log, dt_bias, o_gain)
    return jnp.concatenate([yc, yd], axis=-1) @ w_out


def peer_ffn(x, w_q, sub_keys, u_tab, v_tab):
    b, s, d = x.shape
    xt = x.reshape(b * s, d)
    q = (xt @ w_q).astype(F32).reshape(-1, PEER_HEADS, 2, PEER_QDIM // 2)
    scores = jnp.einsum('thpd,pkd->thpk', q, sub_keys.astype(F32))
    sv, si = lax.top_k(scores, PEER_TOPK)
    cand = (sv[:, :, 0, :, None] + sv[:, :, 1, None, :]).reshape(-1, PEER_HEADS, PEER_TOPK * PEER_TOPK)
    cv, ci = lax.top_k(cand, PEER_TOPK)
    ia = jnp.take_along_axis(si[:, :, 0], ci // PEER_TOPK, axis=-1)
    ib = jnp.take_along_axis(si[:, :, 1], ci % PEER_TOPK, axis=-1)
    experts = ia * PEER_KEYS + ib
    gates = jax.nn.softmax(cv, axis=-1).astype(x.dtype)
    nblk = (b * s) // PEER_BLOCK

    def block(args):
        xb, eb, gb = args
        hid = jnp.einsum('td,thkd->thk', xb, jnp.take(u_tab, eb, axis=0))
        act = jax.nn.gelu(hid) * gb
        return jnp.einsum('thk,thkd->td', act, jnp.take(v_tab, eb, axis=0))

    out = lax.map(block, (xt.reshape(nblk, PEER_BLOCK, d),
                          experts.reshape(nblk, PEER_BLOCK, PEER_HEADS, PEER_TOPK),
                          gates.reshape(nblk, PEER_BLOCK, PEER_HEADS, PEER_TOPK)))
    return out.reshape(b, s, d)


def setup_inputs(seed: int = 0) -> dict:
    key = jax.random.key(seed)
    ks = iter(jax.random.split(key, 40))
    nrm = lambda shape, scale: jax.random.normal(next(ks), shape, F32) * scale
    gain = lambda shape: 1.0 + 0.1 * jax.random.normal(next(ks), shape, F32)
    x = jax.random.normal(next(ks), (BATCH, SEQ, D_MODEL), F32)
    dt0 = jnp.exp(jax.random.uniform(next(ks), (N_ODD, GDN_HEADS), F32) * (np.log(0.1) - np.log(0.001)) + np.log(0.001))
    return {
        "x": x,
        "ab_norm": gain((N_EVEN, D_MODEL)),
        "ab_w_in": nrm((N_EVEN, D_MODEL, AB_IN), D_MODEL ** -0.5),
        "ab_conv": nrm((N_EVEN, HALF, CONV_A_WIDTH), CONV_A_WIDTH ** -0.5),
        "ab_ln_g": gain((N_EVEN, HALF)),
        "ab_ln_b": nrm((N_EVEN, HALF), 0.02),
        "ab_w_s": nrm((N_EVEN, SGU_GROUPS, SGU_BLOCK, SGU_BLOCK), SGU_BLOCK ** -0.5),
        "ab_b_s": nrm((N_EVEN, SGU_GROUPS, SGU_BLOCK), 0.02),
        "ab_w_out": nrm((N_EVEN, MIX_WIDTH, D_MODEL), MIX_WIDTH ** -0.5),
        "cd_norm": gain((N_ODD, D_MODEL)),
        "cd_w_in": nrm((N_ODD, D_MODEL, CD_IN), D_MODEL ** -0.5),
        "cd_pool_w": nrm((N_ODD, len(POOL_WINDOWS), POOL_GROUP, POOL_GROUP), POOL_GROUP ** -0.5),
        "cd_pool_scale": gain((N_ODD, HALF)),
        "cd_conv_qkv": nrm((N_ODD, 3 * HALF, GDN_CONV), GDN_CONV ** -0.5),
        "cd_a_log": jnp.log(jax.random.uniform(next(ks), (N_ODD, GDN_HEADS), F32, 1.0, 16.0)),
        "cd_dt_bias": dt0 + jnp.log(-jnp.expm1(-dt0)),
        "cd_o_gain": gain((N_ODD, GDN_DV)),
        "cd_w_out": nrm((N_ODD, MIX_WIDTH, D_MODEL), MIX_WIDTH ** -0.5),
        "ffn_norm": gain((DEPTH, D_MODEL)),
        "peer_wq": nrm((DEPTH, D_MODEL, PEER_HEADS * PEER_QDIM), D_MODEL ** -0.5),
        "peer_keys": nrm((DEPTH, 2, PEER_KEYS, PEER_QDIM // 2), (PEER_QDIM // 2) ** -0.5),
        "peer_u": nrm((DEPTH, PEER_EXPERTS, D_MODEL), D_MODEL ** -0.5),
        "peer_v": nrm((DEPTH, PEER_EXPERTS, D_MODEL), PEER_HEADS ** -0.5),
        "final_norm": gain((D_MODEL,)),
    }


def reference(x, ab_norm, ab_w_in, ab_conv, ab_ln_g, ab_ln_b, ab_w_s, ab_b_s, ab_w_out,
              cd_norm, cd_w_in, cd_pool_w, cd_pool_scale, cd_conv_qkv, cd_a_log, cd_dt_bias,
              cd_o_gain, cd_w_out, ffn_norm, peer_wq, peer_keys, peer_u, peer_v, final_norm):
    h = x
    for layer in range(DEPTH):
        i = layer // 2
        if layer % 2 == 0:
            h = h + mixer_ab(rmsnorm(h, ab_norm[i]), ab_w_in[i], ab_conv[i], ab_ln_g[i], ab_ln_b[i],
                             ab_w_s[i], ab_b_s[i], ab_w_out[i])
        else:
            h = h + mixer_cd(rmsnorm(h, cd_norm[i]), cd_w_in[i], cd_pool_w[i], cd_pool_scale[i],
                             cd_conv_qkv[i], cd_a_log[i], cd_dt_bias[i], cd_o_gain[i], cd_w_out[i])
        h = h + peer_ffn(rmsnorm(h, ffn_norm[layer]), peer_wq[layer], peer_keys[layer],
                         peer_u[layer], peer_v[layer])
    return rmsnorm(h, final_norm)
```

```python
import functools
import math

import jax
import jax.numpy as jnp
from jax import lax
from jax.experimental import pallas as pl
from jax.experimental.pallas import tpu as pltpu

F32 = jnp.float32
BF16 = jnp.bfloat16
EPS = 1e-6

D_MODEL = 4096
HALF = D_MODEL // 2
CHUNK = 64
SGU_BLOCK = 128
SGU_GROUPS = 8
SGU_GROUP_WIDTH = HALF // SGU_GROUPS
POOL_WINDOWS = (2, 4, 8, 16)
POOL_GROUP = HALF // len(POOL_WINDOWS)
GDN_HEADS = 16
GDN_DK = HALF // GDN_HEADS
GDN_CONV = 4
PEER_HEADS = 8
PEER_KEYS = 128
PEER_TOPK = 16
PEER_PICKS = PEER_HEADS * PEER_TOPK
LANES = 128
SUBLANES = 8
VMEM_LIMIT_BYTES = 56 * 1024 * 1024
HI = lax.Precision.HIGHEST


def _params(*semantics):
    return pltpu.CompilerParams(dimension_semantics=semantics, vmem_limit_bytes=VMEM_LIMIT_BYTES)


def _gelu(x):
    return 0.5 * x * (1.0 + jnp.tanh(math.sqrt(2.0 / math.pi) * (x + 0.044715 * (x * x * x))))


def _silu(x):
    return x * (1.0 / (1.0 + jnp.exp(-x)))


def _softplus(x):
    return jnp.maximum(x, 0.0) + jnp.log(1.0 + jnp.exp(-jnp.abs(x)))


def _rmsnorm_kernel(x_ref, g_ref, *o_refs):
    x = x_ref[...]
    y = x * lax.rsqrt(jnp.mean(x * x, axis=-1, keepdims=True) + EPS) * g_ref[...]
    for o_ref in o_refs:
        o_ref[...] = y.astype(o_ref.dtype)


def _rmsnorm(x, g, out_dtypes, rows=256):
    t, d = x.shape
    outs = pl.pallas_call(
        _rmsnorm_kernel,
        grid=(t // rows,),
        in_specs=[pl.BlockSpec((rows, d), lambda i: (i, 0)), pl.BlockSpec((1, d), lambda i: (0, 0))],
        out_specs=[pl.BlockSpec((rows, d), lambda i: (i, 0)) for _ in out_dtypes],
        out_shape=[jax.ShapeDtypeStruct((t, d), dt) for dt in out_dtypes],
        compiler_params=_params("parallel"),
        name="rmsnorm",
    )(x, g.reshape(1, d))
    return outs


def _matmul_kernel(x_ref, w_ref, *rest):
    o_ref = rest[-1]
    acc = jnp.dot(x_ref[...], w_ref[...].astype(BF16), preferred_element_type=F32)
    if len(rest) == 2:
        acc = acc + rest[0][...]
    o_ref[...] = acc


def _matmul(x, w, n_out, residual=None, tm=1024, tn=512):
    m, k = x.shape
    tm = min(tm, m)
    tn = min(tn, n_out)
    in_specs = [pl.BlockSpec((tm, k), lambda i, j: (i, 0)), pl.BlockSpec((k, tn), lambda i, j: (0, j))]
    args = [x, w]
    if residual is not None:
        in_specs.append(pl.BlockSpec((tm, tn), lambda i, j: (i, j)))
        args.append(residual)
    return pl.pallas_call(
        _matmul_kernel,
        grid=(m // tm, n_out // tn),
        in_specs=in_specs,
        out_specs=pl.BlockSpec((tm, tn), lambda i, j: (i, j)),
        out_shape=jax.ShapeDtypeStruct((m, n_out), F32),
        compiler_params=_params("parallel", "arbitrary"),
        name="matmul",
    )(*args)


def _mixer_ab_kernel(p_ref, hc_ref, hh_ref, cw_ref, lng_ref, lnb_ref, ws_ref, bs_ref, o_ref, *, blocks_per_seq):
    first = (pl.program_id(0) % blocks_per_seq) == 0
    gate_b = p_ref[:, 0:HALF]
    z = p_ref[:, HALF:2 * HALF] * p_ref[:, 2 * HALF:3 * HALF]
    z_halo = jnp.where(first, 0.0, hc_ref[...] * hh_ref[...])
    zz = jnp.concatenate([z_halo, z], axis=0)
    z1 = pltpu.roll(zz, 1, axis=0)[SUBLANES:]
    z2 = pltpu.roll(zz, 2, axis=0)[SUBLANES:]
    conv = cw_ref[0:1, :] * z2 + cw_ref[1:2, :] * z1 + cw_ref[2:3, :] * z
    o_ref[:, 0:HALF] = (gate_b * conv).astype(o_ref.dtype)

    u = _gelu(p_ref[:, 3 * HALF:4 * HALF])
    v = _gelu(p_ref[:, 4 * HALF:5 * HALF])
    mu = jnp.mean(v, axis=-1, keepdims=True)
    vc = v - mu
    var = jnp.mean(vc * vc, axis=-1, keepdims=True)
    vn = vc * lax.rsqrt(var + EPS) * lng_ref[...] + lnb_ref[...]
    row = lax.broadcasted_iota(jnp.int32, (SGU_BLOCK, SGU_BLOCK), 0)
    col = lax.broadcasted_iota(jnp.int32, (SGU_BLOCK, SGU_BLOCK), 1)
    for g in range(SGU_GROUPS):
        cols = slice(g * SGU_GROUP_WIDTH, (g + 1) * SGU_GROUP_WIDTH)
        w = jnp.where(row >= col, ws_ref[g], 0.0).astype(BF16)
        mixed = jnp.dot(w, vn[:, cols].astype(BF16), preferred_element_type=F32) + bs_ref[:, g:g + 1]
        o_ref[:, HALF + g * SGU_GROUP_WIDTH:HALF + (g + 1) * SGU_GROUP_WIDTH] = (u[:, cols] * mixed).astype(o_ref.dtype)


def _mixer_ab(p, seq, conv_w, ln_g, ln_b, w_s, b_s):
    t = p.shape[0]
    rows = SGU_BLOCK
    halo_blocks = rows // SUBLANES

    def halo_map(col_block):
        return lambda i: (jnp.maximum(i * halo_blocks - 1, 0), col_block)

    return pl.pallas_call(
        functools.partial(_mixer_ab_kernel, blocks_per_seq=seq // rows),
        grid=(t // rows,),
        in_specs=[
            pl.BlockSpec((rows, 5 * HALF), lambda i: (i, 0)),
            pl.BlockSpec((SUBLANES, HALF), halo_map(1)),
            pl.BlockSpec((SUBLANES, HALF), halo_map(2)),
            pl.BlockSpec((3, HALF), lambda i: (0, 0)),
            pl.BlockSpec((1, HALF), lambda i: (0, 0)),
            pl.BlockSpec((1, HALF), lambda i: (0, 0)),
            pl.BlockSpec((SGU_GROUPS, SGU_BLOCK, SGU_BLOCK), lambda i: (0, 0, 0)),
            pl.BlockSpec((SGU_BLOCK, SGU_GROUPS), lambda i: (0, 0)),
        ],
        out_specs=pl.BlockSpec((rows, D_MODEL), lambda i: (i, 0)),
        out_shape=jax.ShapeDtypeStruct((t, D_MODEL), BF16),
        compiler_params=_params("parallel"),
        name="mixer_ab",
    )(p, p, p, conv_w.T, ln_g.reshape(1, HALF), ln_b.reshape(1, HALF), w_s, b_s.T)


def _mixer_c_kernel(p_ref, halo_ref, w_ref, scale_ref, o_ref, *, rows, blocks_per_seq):
    blk = pl.program_id(0) % blocks_per_seq
    halo_rows = halo_ref.shape[0]
    x = p_ref[...]
    halo = jnp.where(blk == 0, 0.0, halo_ref[...])
    s = jnp.concatenate([halo, x], axis=0)
    pos = blk * rows + lax.broadcasted_iota(jnp.int32, (rows, 1), 0)
    shift = 1
    for g, win in enumerate(POOL_WINDOWS):
        s = s[:, (POOL_GROUP if g else 0):]
        while shift < win:
            s = s + pltpu.roll(s, shift, axis=0)
            shift *= 2
        cnt = jnp.minimum(pos + 1, win).astype(F32)
        xg = x[:, g * POOL_GROUP:(g + 1) * POOL_GROUP]
        pooled = s[halo_rows:, :POOL_GROUP] / cnt - xg
        y = jnp.dot(pooled.astype(BF16), w_ref[g].astype(BF16), preferred_element_type=F32)
        o_ref[:, g * POOL_GROUP:(g + 1) * POOL_GROUP] = (y * scale_ref[:, g * POOL_GROUP:(g + 1) * POOL_GROUP]).astype(o_ref.dtype)


def _mixer_c(p, seq, pool_w, pool_scale, rows=256):
    t = p.shape[0]
    halo_rows = max(POOL_WINDOWS)
    return pl.pallas_call(
        functools.partial(_mixer_c_kernel, rows=rows, blocks_per_seq=seq // rows),
        grid=(t // rows,),
        in_specs=[
            pl.BlockSpec((rows, HALF), lambda i: (i, 0)),
            pl.BlockSpec((halo_rows, HALF), lambda i: (jnp.maximum(i * (rows // halo_rows) - 1, 0), 0)),
            pl.BlockSpec((len(POOL_WINDOWS), POOL_GROUP, POOL_GROUP), lambda i: (0, 0, 0)),
            pl.BlockSpec((1, HALF), lambda i: (0, 0)),
        ],
        out_specs=pl.BlockSpec((rows, HALF), lambda i: (i, 0)),
        out_shape=jax.ShapeDtypeStruct((t, HALF), BF16),
        compiler_params=_params("parallel"),
        name="mixer_c",
    )(p, p, pool_w, pool_scale.reshape(1, HALF))


def _gdn_prep_kernel(x_ref, halo_ref, cw_ref, o_ref, *, blocks_per_seq):
    first = (pl.program_id(0) % blocks_per_seq) == 0
    section = pl.program_id(1)
    x = x_ref[...]
    halo = jnp.where(first, 0.0, halo_ref[...])
    xx = jnp.concatenate([halo, x], axis=0)
    acc = cw_ref[GDN_CONV - 1:GDN_CONV, :] * x
    for lag in range(1, GDN_CONV):
        acc = acc + cw_ref[GDN_CONV - 1 - lag:GDN_CONV - lag, :] * pltpu.roll(xx, lag, axis=0)[SUBLANES:]
    act = _silu(acc)
    q_scale = jnp.where(section == 0, GDN_DK ** -0.5, 1.0)
    for h in range(GDN_HEADS):
        a = act[:, h * GDN_DK:(h + 1) * GDN_DK]
        normed = a * (lax.rsqrt(jnp.sum(a * a, axis=-1, keepdims=True) + EPS) * q_scale)
        o_ref[:, h * GDN_DK:(h + 1) * GDN_DK] = jnp.where(section < 2, normed, a)


def _gdn_prep(p, seq, conv_w, rows=256):
    t = p.shape[0]
    return pl.pallas_call(
        functools.partial(_gdn_prep_kernel, blocks_per_seq=seq // rows),
        grid=(t // rows, 3),
        in_specs=[
            pl.BlockSpec((rows, HALF), lambda i, j: (i, 1 + j)),
            pl.BlockSpec((SUBLANES, HALF), lambda i, j: (jnp.maximum(i * (rows // SUBLANES) - 1, 0), 1 + j)),
            pl.BlockSpec((GDN_CONV, HALF), lambda i, j: (0, j)),
        ],
        out_specs=pl.BlockSpec((rows, HALF), lambda i, j: (i, j)),
        out_shape=jax.ShapeDtypeStruct((t, 3 * HALF), F32),
        compiler_params=_params("parallel", "arbitrary"),
        name="gdn_prep",
    )(p, p, conv_w.T)


GDN_GROUP = 4


def _gdn_kernel(alog_ref, dtb_ref, q_ref, k_ref, v_ref, z_ref, ab_ref, arow_ref, gain_ref, o_ref,
                u_s, w_s, qd_s, kd_s, attn_s, egl_s, *, seq):
    h = pl.program_id(1)
    n_chunks = seq // CHUNK
    c = CHUNK
    neg_a = -jnp.exp(jnp.full((1, 1), alog_ref[h], F32))
    dt_bias = dtb_ref[h]
    ii = lax.broadcasted_iota(jnp.int32, (c, c), 0)
    jj = lax.broadcasted_iota(jnp.int32, (c, c), 1)
    lower = (ii >= jj)[None]
    strict = (ii > jj)[None]
    eye = (ii == jj).astype(F32)[None]
    lane = lax.broadcasted_iota(jnp.int32, (1, 1, LANES), 2)

    def local(grp, carry):
        c0 = pl.multiple_of(grp * (GDN_GROUP * c), GDN_GROUP * c)
        rows = pl.ds(c0, GDN_GROUP * c)
        q = q_ref[0, rows, :].reshape(GDN_GROUP, c, GDN_DK)
        k = k_ref[0, rows, :].reshape(GDN_GROUP, c, GDN_DK)
        v = v_ref[0, rows, :].reshape(GDN_GROUP, c, GDN_DK)
        ab = ab_ref[0, rows, :].reshape(GDN_GROUP, c, LANES)
        a_col = jnp.sum(jnp.where(lane == h, ab, 0.0), axis=-1, keepdims=True)
        b_col = jnp.sum(jnp.where(lane == h + GDN_HEADS, ab, 0.0), axis=-1, keepdims=True)
        a_row = arow_ref[0, 0, pl.ds(grp * GDN_GROUP, GDN_GROUP), :, :]
        g_col = neg_a * _softplus(a_col + dt_bias)
        g_row = neg_a * _softplus(a_row + dt_bias)
        beta = 1.0 / (1.0 + jnp.exp(-b_col))
        gc_col = jnp.sum(jnp.where(lower, g_row, 0.0), axis=-1, keepdims=True)
        gc_row = jnp.sum(jnp.where(strict, 0.0, g_col), axis=1, keepdims=True)
        decay = jnp.where(lower, jnp.exp(jnp.where(lower, gc_col - gc_row, 0.0)), 0.0)
        kb = k * beta
        kk = jnp.einsum('gid,gjd->gij', kb, k, precision=HI, preferred_element_type=F32)
        n = jnp.where(strict, kk * decay, 0.0)
        t_inv = eye - n
        pw = jnp.einsum('gij,gjk->gik', n, n, precision=HI, preferred_element_type=F32)
        span = 2
        while True:
            t_inv = t_inv + jnp.einsum('gij,gjk->gik', t_inv, pw, precision=HI, preferred_element_type=F32)
            span *= 2
            if span >= c:
                break
            pw = jnp.einsum('gij,gjk->gik', pw, pw, precision=HI, preferred_element_type=F32)
        e_gc = jnp.exp(gc_col)
        u = jnp.einsum('gij,gjd->gid', t_inv, v * beta, precision=HI, preferred_element_type=F32)
        w = jnp.einsum('gij,gjd->gid', t_inv, kb * e_gc, precision=HI, preferred_element_type=F32)
        qk = jnp.einsum('gid,gjd->gij', q, k, precision=HI, preferred_element_type=F32)
        g_last = gc_col[:, c - 1:c, :]
        u_s[rows, :] = u.reshape(GDN_GROUP * c, GDN_DK)
        w_s[rows, :] = w.reshape(GDN_GROUP * c, GDN_DK)
        qd_s[rows, :] = (q * e_gc).reshape(GDN_GROUP * c, GDN_DK)
        kd_s[rows, :] = (k * jnp.exp(g_last - gc_col)).reshape(GDN_GROUP * c, GDN_DK)
        attn_s[rows, :] = (qk * decay).reshape(GDN_GROUP * c, c)
        egl_s[pl.ds(grp * GDN_GROUP, GDN_GROUP)] = jnp.broadcast_to(jnp.exp(g_last), (GDN_GROUP, 1, GDN_DK))
        return carry

    lax.fori_loop(0, n_chunks // GDN_GROUP, local, 0)

    def scan(ci, state):
        rows = pl.ds(pl.multiple_of(ci * c, c), c)
        v_new = u_s[rows, :] - jnp.dot(w_s[rows, :], state, precision=HI, preferred_element_type=F32)
        o = (jnp.dot(qd_s[rows, :], state, precision=HI, preferred_element_type=F32)
             + jnp.dot(attn_s[rows, :], v_new, precision=HI, preferred_element_type=F32))
        state = state * egl_s[ci] + lax.dot_general(kd_s[rows, :], v_new, (((0,), (0,)), ((), ())),
                                                    precision=HI, preferred_element_type=F32)
        o = o * lax.rsqrt(jnp.mean(o * o, axis=-1, keepdims=True) + EPS) * gain_ref[...]
        o_ref[0, rows, :] = (o * _silu(z_ref[0, rows, :])).astype(o_ref.dtype)
        return state

    lax.fori_loop(0, n_chunks, scan, jnp.zeros((GDN_DK, GDN_DK), F32))


def _gdn(qkv, p3, ab3, a_rows, a_log, dt_bias, o_gain):
    b, seq, _ = qkv.shape
    hblk = HALF // GDN_DK
    smem = pl.BlockSpec(memory_space=pltpu.SMEM)

    def head(col_block0):
        return pl.BlockSpec((1, seq, GDN_DK), lambda bi, hi: (bi, 0, col_block0 + hi))

    return pl.pallas_call(
        functools.partial(_gdn_kernel, seq=seq),
        grid=(b, GDN_HEADS),
        in_specs=[
            smem, smem,
            head(0), head(hblk), head(2 * hblk), head(4 * hblk),
            pl.BlockSpec((1, seq, LANES), lambda bi, hi: (bi, 0, 0)),
            pl.BlockSpec((1, 1, seq // CHUNK, 1, CHUNK), lambda bi, hi: (bi, hi, 0, 0, 0)),
            pl.BlockSpec((1, GDN_DK), lambda bi, hi: (0, 0)),
        ],
        out_specs=pl.BlockSpec((1, seq, GDN_DK), lambda bi, hi: (bi, 0, hi)),
        out_shape=jax.ShapeDtypeStruct((b, seq, HALF), BF16),
        scratch_shapes=[
            pltpu.VMEM((seq, GDN_DK), F32), pltpu.VMEM((seq, GDN_DK), F32),
            pltpu.VMEM((seq, GDN_DK), F32), pltpu.VMEM((seq, GDN_DK), F32),
            pltpu.VMEM((seq, CHUNK), F32), pltpu.VMEM((seq // CHUNK, 1, GDN_DK), F32),
        ],
        compiler_params=_params("parallel", "arbitrary"),
        name="gdn",
    )(a_log, dt_bias, qkv, qkv, qkv, p3, ab3, a_rows, o_gain.reshape(1, GDN_DK))


def _top_rows(s, k, payload=None):
    n = s.shape[0]
    row = lax.broadcasted_iota(jnp.int32, s.shape, 0)
    vals, picks = [], []
    for _ in range(k):
        m = jnp.max(s, axis=0, keepdims=True)
        idx = jnp.min(jnp.where(s == m, row, n), axis=0, keepdims=True)
        hit = row == idx
        vals.append(m)
        picks.append(idx if payload is None else jnp.max(jnp.where(hit, payload, -1), axis=0, keepdims=True))
        s = jnp.where(hit, -jnp.inf, s)
    return jnp.concatenate(vals, axis=0), jnp.concatenate(picks, axis=0)


def _peer_select_kernel(q_ref, keys_ref, e_ref, g_ref):
    half_q = PEER_KEYS
    for h in range(PEER_HEADS):
        sv, si = [], []
        for p in range(2):
            qc = q_ref[:, (2 * h + p) * half_q:(2 * h + p + 1) * half_q]
            scores_t = lax.dot_general(keys_ref[p], qc, (((1,), (1,)), ((), ())),
                                       precision=HI, preferred_element_type=F32)
            v, i = _top_rows(scores_t, PEER_TOPK)
            sv.append(v)
            si.append(i)
        cand = jnp.concatenate([sv[0][i:i + 1] + sv[1] for i in range(PEER_TOPK)], axis=0)
        cand_e = jnp.concatenate([si[0][i:i + 1] * PEER_KEYS + si[1] for i in range(PEER_TOPK)], axis=0)
        cv, ce = _top_rows(cand, PEER_TOPK, payload=cand_e)
        ex = jnp.exp(cv - cv[0:1])
        e_ref[h * PEER_TOPK:(h + 1) * PEER_TOPK, :] = ce
        g_ref[h * PEER_TOPK:(h + 1) * PEER_TOPK, :] = ex / jnp.sum(ex, axis=0, keepdims=True)


def _peer_select(q, keys, tokens=256):
    t = q.shape[0]
    return pl.pallas_call(
        _peer_select_kernel,
        grid=(t // tokens,),
        in_specs=[pl.BlockSpec((tokens, q.shape[1]), lambda i: (i, 0)),
                  pl.BlockSpec(keys.shape, lambda i: (0, 0, 0))],
        out_specs=[pl.BlockSpec((PEER_PICKS, tokens), lambda i: (0, i)),
                   pl.BlockSpec((PEER_PICKS, tokens), lambda i: (0, i))],
        out_shape=[jax.ShapeDtypeStruct((PEER_PICKS, t), jnp.int32),
                   jax.ShapeDtypeStruct((PEER_PICKS, t), F32)],
        compiler_params=_params("parallel"),
        name="peer_select",
    )(q, keys)


PEER_TOKENS = 32
PEER_SLOTS = 3


def _peer_gather_kernel(e_ref, x_ref, gt_ref, r_ref, u_all, v_all, o_ref, ubuf, vbuf, usem, vsem, *, layer):
    u_hbm = u_all.at[layer]
    v_hbm = v_all.at[layer]

    def row_copies(tok, slot, j):
        e = e_ref[tok, j]
        return (pltpu.make_async_copy(u_hbm.at[pl.ds(e, 1), :], ubuf.at[slot, pl.ds(j, 1), :], usem.at[slot]),
                pltpu.make_async_copy(v_hbm.at[pl.ds(e, 1), :], vbuf.at[slot, pl.ds(j, 1), :], vsem.at[slot]))

    def start_token(tok, slot):
        def body(j, carry):
            cu, cv = row_copies(tok, slot, j)
            cu.start()
            cv.start()
            return carry
        lax.fori_loop(0, PEER_PICKS, body, 0)

    def wait_token(slot):
        pltpu.make_async_copy(u_hbm.at[pl.ds(0, PEER_PICKS), :], ubuf.at[slot], usem.at[slot]).wait()
        pltpu.make_async_copy(v_hbm.at[pl.ds(0, PEER_PICKS), :], vbuf.at[slot], vsem.at[slot]).wait()

    for s in range(PEER_SLOTS - 1):
        start_token(s, s)
    lane = lax.broadcasted_iota(jnp.int32, (PEER_PICKS, PEER_TOKENS), 1)

    def token(tok, carry):
        slot = tok % PEER_SLOTS
        nxt = tok + PEER_SLOTS - 1

        @pl.when(nxt < PEER_TOKENS)
        def _():
            start_token(nxt, nxt % PEER_SLOTS)

        wait_token(slot)
        x = x_ref[pl.ds(tok, 1), :]
        hid = jnp.sum(ubuf[slot] * x, axis=-1, keepdims=True)
        gate = jnp.sum(jnp.where(lane == tok, gt_ref[0], 0.0), axis=-1, keepdims=True)
        act = _gelu(hid) * gate
        y = jnp.sum(vbuf[slot] * act, axis=0, keepdims=True)
        o_ref[pl.ds(tok, 1), :] = r_ref[pl.ds(tok, 1), :] + y
        return carry

    lax.fori_loop(0, PEER_TOKENS, token, 0)


def _peer_gather(experts, xn, gates_t, resid, u_tabs, v_tabs, layer):
    t, d = xn.shape
    blk = PEER_TOKENS
    return pl.pallas_call(
        functools.partial(_peer_gather_kernel, layer=layer),
        grid=(t // blk,),
        in_specs=[
            pl.BlockSpec((blk, PEER_PICKS), lambda i: (i, 0), memory_space=pltpu.SMEM),
            pl.BlockSpec((blk, d), lambda i: (i, 0)),
            pl.BlockSpec((1, PEER_PICKS, blk), lambda i: (i, 0, 0)),
            pl.BlockSpec((blk, d), lambda i: (i, 0)),
            pl.BlockSpec(memory_space=pl.ANY),
            pl.BlockSpec(memory_space=pl.ANY),
        ],
        out_specs=pl.BlockSpec((blk, d), lambda i: (i, 0)),
        out_shape=jax.ShapeDtypeStruct((t, d), F32),
        scratch_shapes=[
            pltpu.VMEM((PEER_SLOTS, PEER_PICKS, d), F32),
            pltpu.VMEM((PEER_SLOTS, PEER_PICKS, d), F32),
            pltpu.SemaphoreType.DMA((PEER_SLOTS,)),
            pltpu.SemaphoreType.DMA((PEER_SLOTS,)),
        ],
        compiler_params=_params("arbitrary"),
        name="peer_gather",
    )(experts, xn, gates_t, resid, u_tabs, v_tabs)


def _peer_ffn(h, norm_g, w_q, keys, u_tabs, v_tabs, layer):
    t = h.shape[0]
    xn, xn_bf = _rmsnorm(h, norm_g, (F32, BF16))
    q = _matmul(xn_bf, w_q, w_q.shape[1])
    experts_t, gates_t = _peer_select(q, keys)
    experts = experts_t.T
    gates_blk = gates_t.reshape(PEER_PICKS, t // PEER_TOKENS, PEER_TOKENS).transpose(1, 0, 2)
    return _peer_gather(experts, xn, gates_blk, h, u_tabs, v_tabs, layer)


def kernel(x, ab_norm, ab_w_in, ab_conv, ab_ln_g, ab_ln_b, ab_w_s, ab_b_s, ab_w_out, cd_norm, cd_w_in, cd_pool_w,
           cd_pool_scale, cd_conv_qkv, cd_a_log, cd_dt_bias, cd_o_gain, cd_w_out, ffn_norm, peer_wq, peer_keys,
           peer_u, peer_v, final_norm):
    b, seq, d = x.shape
    t = b * seq
    h = x.reshape(t, d)

    (xn,) = _rmsnorm(h, ab_norm[0], (BF16,))
    p = _matmul(xn, ab_w_in[0], 5 * HALF)
    y = _mixer_ab(p, seq, ab_conv[0], ab_ln_g[0], ab_ln_b[0], ab_w_s[0], ab_b_s[0])
    h = _matmul(y, ab_w_out[0], d, residual=h)
    h = _peer_ffn(h, ffn_norm[0], peer_wq[0], peer_keys[0], peer_u, peer_v, 0)

    (xn,) = _rmsnorm(h, cd_norm[0], (BF16,))
    p = _matmul(xn, cd_w_in[0], 5 * HALF)
    w_gates = jnp.pad(cd_w_in[0][:, 5 * HALF:], ((0, 0), (0, LANES - 2 * GDN_HEADS)))
    ab = _matmul(xn, w_gates, LANES)
    yc = _mixer_c(p, seq, cd_pool_w[0], cd_pool_scale[0])
    qkv = _gdn_prep(p, seq, cd_conv_qkv[0])
    ab3 = ab.reshape(b, seq, LANES)
    a_rows = ab3[:, :, :GDN_HEADS].transpose(0, 2, 1).reshape(b, GDN_HEADS, seq // CHUNK, 1, CHUNK)
    yd = _gdn(qkv.reshape(b, seq, 3 * HALF), p.reshape(b, seq, 5 * HALF), ab3, a_rows,
              cd_a_log[0], cd_dt_bias[0], cd_o_gain[0])
    y = jnp.concatenate([yc, yd.reshape(t, HALF)], axis=1)
    h = _matmul(y, cd_w_out[0], d, residual=h)
    h = _peer_ffn(h, ffn_norm[1], peer_wq[1], peer_keys[1], peer_u, peer_v, 1)

    (out,) = _rmsnorm(h, final_norm, (F32,))
    return out.reshape(b, seq, d)
```

```python
import functools
import math

import jax
import jax.numpy as jnp
from jax import lax
from jax.experimental import pallas as pl
from jax.experimental.pallas import tpu as pltpu

F32 = jnp.float32
BF16 = jnp.bfloat16
EPS = 1e-6

D_MODEL = 4096
HALF = D_MODEL // 2
CHUNK = 64
SGU_BLOCK = 128
SGU_GROUPS = 8
SGU_GROUP_WIDTH = HALF // SGU_GROUPS
POOL_WINDOWS = (2, 4, 8, 16)
POOL_GROUP = HALF // len(POOL_WINDOWS)
GDN_HEADS = 16
GDN_DK = HALF // GDN_HEADS
GDN_CONV = 4
PEER_HEADS = 8
PEER_KEYS = 128
PEER_TOPK = 16
PEER_PICKS = PEER_HEADS * PEER_TOPK
LANES = 128
SUBLANES = 8
VMEM_LIMIT_BYTES = 56 * 1024 * 1024
HI = lax.Precision.HIGHEST


def _params(*semantics):
    return pltpu.CompilerParams(dimension_semantics=semantics, vmem_limit_bytes=VMEM_LIMIT_BYTES)


def _gelu(x):
    return 0.5 * x * (1.0 + jnp.tanh(math.sqrt(2.0 / math.pi) * (x + 0.044715 * (x * x * x))))


def _silu(x):
    return x * (1.0 / (1.0 + jnp.exp(-x)))


def _softplus(x):
    return jnp.maximum(x, 0.0) + jnp.log(1.0 + jnp.exp(-jnp.abs(x)))


def _rmsnorm_kernel(x_ref, g_ref, *o_refs):
    x = x_ref[...]
    y = x * lax.rsqrt(jnp.mean(x * x, axis=-1, keepdims=True) + EPS) * g_ref[...]
    for o_ref in o_refs:
        o_ref[...] = y.astype(o_ref.dtype)


def _rmsnorm(x, g, out_dtypes, rows=256):
    t, d = x.shape
    outs = pl.pallas_call(
        _rmsnorm_kernel,
        grid=(t // rows,),
        in_specs=[pl.BlockSpec((rows, d), lambda i: (i, 0)), pl.BlockSpec((1, d), lambda i: (0, 0))],
        out_specs=[pl.BlockSpec((rows, d), lambda i: (i, 0)) for _ in out_dtypes],
        out_shape=[jax.ShapeDtypeStruct((t, d), dt) for dt in out_dtypes],
        compiler_params=_params("parallel"),
        name="rmsnorm",
    )(x, g.reshape(1, d))
    return outs


def _matmul_kernel(x_ref, w_ref, *rest):
    o_ref = rest[-1]
    acc = jnp.dot(x_ref[...], w_ref[...].astype(BF16), preferred_element_type=F32)
    if len(rest) == 2:
        acc = acc + rest[0][...]
    o_ref[...] = acc


def _matmul(x, w, n_out, residual=None, tm=1024, tn=512):
    m, k = x.shape
    tm = min(tm, m)
    tn = min(tn, n_out)
    in_specs = [pl.BlockSpec((tm, k), lambda i, j: (i, 0)), pl.BlockSpec((k, tn), lambda i, j: (0, j))]
    args = [x, w]
    if residual is not None:
        in_specs.append(pl.BlockSpec((tm, tn), lambda i, j: (i, j)))
        args.append(residual)
    return pl.pallas_call(
        _matmul_kernel,
        grid=(m // tm, n_out // tn),
        in_specs=in_specs,
        out_specs=pl.BlockSpec((tm, tn), lambda i, j: (i, j)),
        out_shape=jax.ShapeDtypeStruct((m, n_out), F32),
        compiler_params=_params("parallel", "arbitrary"),
        name="matmul",
    )(*args)


def _mixer_ab_kernel(p_ref, hc_ref, hh_ref, cw_ref, lng_ref, lnb_ref, ws_ref, bs_ref, o_ref, *, blocks_per_seq):
    first = (pl.program_id(0) % blocks_per_seq) == 0
    gate_b = p_ref[:, 0:HALF]
    z = p_ref[:, HALF:2 * HALF] * p_ref[:, 2 * HALF:3 * HALF]
    z_halo = jnp.where(first, 0.0, hc_ref[...] * hh_ref[...])
    zz = jnp.concatenate([z_halo, z], axis=0)
    z1 = pltpu.roll(zz, 1, axis=0)[SUBLANES:]
    z2 = pltpu.roll(zz, 2, axis=0)[SUBLANES:]
    conv = cw_ref[0:1, :] * z2 + cw_ref[1:2, :] * z1 + cw_ref[2:3, :] * z
    o_ref[:, 0:HALF] = (gate_b * conv).astype(o_ref.dtype)

    u = _gelu(p_ref[:, 3 * HALF:4 * HALF])
    v = _gelu(p_ref[:, 4 * HALF:5 * HALF])
    mu = jnp.mean(v, axis=-1, keepdims=True)
    vc = v - mu
    var = jnp.mean(vc * vc, axis=-1, keepdims=True)
    vn = vc * lax.rsqrt(var + EPS) * lng_ref[...] + lnb_ref[...]
    row = lax.broadcasted_iota(jnp.int32, (SGU_BLOCK, SGU_BLOCK), 0)
    col = lax.broadcasted_iota(jnp.int32, (SGU_BLOCK, SGU_BLOCK), 1)
    for g in range(SGU_GROUPS):
        cols = slice(g * SGU_GROUP_WIDTH, (g + 1) * SGU_GROUP_WIDTH)
        w = jnp.where(row >= col, ws_ref[g], 0.0).astype(BF16)
        mixed = jnp.dot(w, vn[:, cols].astype(BF16), preferred_element_type=F32) + bs_ref[:, g:g + 1]
        o_ref[:, HALF + g * SGU_GROUP_WIDTH:HALF + (g + 1) * SGU_GROUP_WIDTH] = (u[:, cols] * mixed).astype(o_ref.dtype)


def _mixer_ab(p, seq, conv_w, ln_g, ln_b, w_s, b_s):
    t = p.shape[0]
    rows = SGU_BLOCK
    halo_blocks = rows // SUBLANES

    def halo_map(col_block):
        return lambda i: (jnp.maximum(i * halo_blocks - 1, 0), col_block)

    return pl.pallas_call(
        functools.partial(_mixer_ab_kernel, blocks_per_seq=seq // rows),
        grid=(t // rows,),
        in_specs=[
            pl.BlockSpec((rows, 5 * HALF), lambda i: (i, 0)),
            pl.BlockSpec((SUBLANES, HALF), halo_map(1)),
            pl.BlockSpec((SUBLANES, HALF), halo_map(2)),
            pl.BlockSpec((3, HALF), lambda i: (0, 0)),
            pl.BlockSpec((1, HALF), lambda i: (0, 0)),
            pl.BlockSpec((1, HALF), lambda i: (0, 0)),
            pl.BlockSpec((SGU_GROUPS, SGU_BLOCK, SGU_BLOCK), lambda i: (0, 0, 0)),
            pl.BlockSpec((SGU_BLOCK, SGU_GROUPS), lambda i: (0, 0)),
        ],
        out_specs=pl.BlockSpec((rows, D_MODEL), lambda i: (i, 0)),
        out_shape=jax.ShapeDtypeStruct((t, D_MODEL), BF16),
        compiler_params=_params("parallel"),
        name="mixer_ab",
    )(p, p, p, conv_w.T, ln_g.reshape(1, HALF), ln_b.reshape(1, HALF), w_s, b_s.T)


def _mixer_c_kernel(p_ref, halo_ref, w_ref, scale_ref, o_ref, *, rows, blocks_per_seq):
    blk = pl.program_id(0) % blocks_per_seq
    halo_rows = halo_ref.shape[0]
    x = p_ref[...]
    halo = jnp.where(blk == 0, 0.0, halo_ref[...])
    s = jnp.concatenate([halo, x], axis=0)
    pos = blk * rows + lax.broadcasted_iota(jnp.int32, (rows, 1), 0)
    shift = 1
    for g, win in enumerate(POOL_WINDOWS):
        s = s[:, (POOL_GROUP if g else 0):]
        while shift < win:
            s = s + pltpu.roll(s, shift, axis=0)
            shift *= 2
        cnt = jnp.minimum(pos + 1, win).astype(F32)
        xg = x[:, g * POOL_GROUP:(g + 1) * POOL_GROUP]
        pooled = s[halo_rows:, :POOL_GROUP] / cnt - xg
        y = jnp.dot(pooled.astype(BF16), w_ref[g].astype(BF16), preferred_element_type=F32)
        o_ref[:, g * POOL_GROUP:(g + 1) * POOL_GROUP] = (y * scale_ref[:, g * POOL_GROUP:(g + 1) * POOL_GROUP]).astype(o_ref.dtype)


def _mixer_c(p, seq, pool_w, pool_scale, rows=256):
    t = p.shape[0]
    halo_rows = max(POOL_WINDOWS)
    return pl.pallas_call(
        functools.partial(_mixer_c_kernel, rows=rows, blocks_per_seq=seq // rows),
        grid=(t // rows,),
        in_specs=[
            pl.BlockSpec((rows, HALF), lambda i: (i, 0)),
            pl.BlockSpec((halo_rows, HALF), lambda i: (jnp.maximum(i * (rows // halo_rows) - 1, 0), 0)),
            pl.BlockSpec((len(POOL_WINDOWS), POOL_GROUP, POOL_GROUP), lambda i: (0, 0, 0)),
            pl.BlockSpec((1, HALF), lambda i: (0, 0)),
        ],
        out_specs=pl.BlockSpec((rows, HALF), lambda i: (i, 0)),
        out_shape=jax.ShapeDtypeStruct((t, HALF), BF16),
        compiler_params=_params("parallel"),
        name="mixer_c",
    )(p, p, pool_w, pool_scale.reshape(1, HALF))


def _gdn_prep_kernel(x_ref, halo_ref, cw_ref, o_ref, *, blocks_per_seq):
    first = (pl.program_id(0) % blocks_per_seq) == 0
    section = pl.program_id(1)
    x = x_ref[...]
    halo = jnp.where(first, 0.0, halo_ref[...])
    xx = jnp.concatenate([halo, x], axis=0)
    acc = cw_ref[GDN_CONV - 1:GDN_CONV, :] * x
    for lag in range(1, GDN_CONV):
        acc = acc + cw_ref[GDN_CONV - 1 - lag:GDN_CONV - lag, :] * pltpu.roll(xx, lag, axis=0)[SUBLANES:]
    act = _silu(acc)
    q_scale = jnp.where(section == 0, GDN_DK ** -0.5, 1.0)
    for h in range(GDN_HEADS):
        a = act[:, h * GDN_DK:(h + 1) * GDN_DK]
        normed = a * (lax.rsqrt(jnp.sum(a * a, axis=-1, keepdims=True) + EPS) * q_scale)
        o_ref[:, h * GDN_DK:(h + 1) * GDN_DK] = jnp.where(section < 2, normed, a)


def _gdn_prep(p, seq, conv_w, rows=256):
    t = p.shape[0]
    return pl.pallas_call(
        functools.partial(_gdn_prep_kernel, blocks_per_seq=seq // rows),
        grid=(t // rows, 3),
        in_specs=[
            pl.BlockSpec((rows, HALF), lambda i, j: (i, 1 + j)),
            pl.BlockSpec((SUBLANES, HALF), lambda i, j: (jnp.maximum(i * (rows // SUBLANES) - 1, 0), 1 + j)),
            pl.BlockSpec((GDN_CONV, HALF), lambda i, j: (0, j)),
        ],
        out_specs=pl.BlockSpec((rows, HALF), lambda i, j: (i, j)),
        out_shape=jax.ShapeDtypeStruct((t, 3 * HALF), F32),
        compiler_params=_params("parallel", "arbitrary"),
        name="gdn_prep",
    )(p, p, conv_w.T)


GDN_GROUP = 8


def _split_bf16(a):
    hi = a.astype(BF16)
    return hi, (a - hi.astype(F32)).astype(BF16)


def _bmm3(a_parts, b_parts):
    mm = functools.partial(jnp.einsum, 'gij,gjk->gik', preferred_element_type=F32)
    return mm(a_parts[0], b_parts[0]) + (mm(a_parts[0], b_parts[1]) + mm(a_parts[1], b_parts[0]))


def _gdn_kernel(alog_ref, dtb_ref, q_ref, k_ref, v_ref, z_ref, ab_ref, arow_ref, gain_ref, o_ref,
                u_s, w_s, qd_s, kd_s, attn_s, egl_s, *, seq):
    h = pl.program_id(1)
    n_chunks = seq // CHUNK
    c = CHUNK
    neg_a = -jnp.exp(jnp.full((1, 1), alog_ref[h], F32))
    dt_bias = dtb_ref[h]
    ii = lax.broadcasted_iota(jnp.int32, (c, c), 0)
    jj = lax.broadcasted_iota(jnp.int32, (c, c), 1)
    lower = (ii >= jj)[None]
    strict = (ii > jj)[None]
    eye = (ii == jj).astype(F32)[None]
    lane = lax.broadcasted_iota(jnp.int32, (1, 1, LANES), 2)

    def local(grp, carry):
        c0 = pl.multiple_of(grp * (GDN_GROUP * c), GDN_GROUP * c)
        rows = pl.ds(c0, GDN_GROUP * c)
        q = q_ref[0, rows, :].reshape(GDN_GROUP, c, GDN_DK)
        k = k_ref[0, rows, :].reshape(GDN_GROUP, c, GDN_DK)
        v = v_ref[0, rows, :].reshape(GDN_GROUP, c, GDN_DK)
        ab = ab_ref[0, rows, :].reshape(GDN_GROUP, c, LANES)
        a_col = jnp.sum(jnp.where(lane == h, ab, 0.0), axis=-1, keepdims=True)
        b_col = jnp.sum(jnp.where(lane == h + GDN_HEADS, ab, 0.0), axis=-1, keepdims=True)
        a_row = arow_ref[0, 0, pl.ds(grp * GDN_GROUP, GDN_GROUP), :, :]
        g_col = neg_a * _softplus(a_col + dt_bias)
        g_row = neg_a * _softplus(a_row + dt_bias)
        beta = 1.0 / (1.0 + jnp.exp(-b_col))
        gc_col = jnp.sum(jnp.where(lower, g_row, 0.0), axis=-1, keepdims=True)
        gc_row = jnp.sum(jnp.where(strict, 0.0, g_col), axis=1, keepdims=True)
        decay = jnp.where(lower, jnp.exp(jnp.where(lower, gc_col - gc_row, 0.0)), 0.0)
        kb = k * beta
        k_bf = k.astype(BF16)
        kk = jnp.einsum('gid,gjd->gij', kb.astype(BF16), k_bf, preferred_element_type=F32)
        n = jnp.where(strict, kk * decay, 0.0)
        t_inv = eye - n
        pw_parts = _split_bf16(n)
        pw = _bmm3(pw_parts, pw_parts)
        span = 2
        while True:
            pw_parts = _split_bf16(pw)
            t_inv = t_inv + _bmm3(_split_bf16(t_inv), pw_parts)
            span *= 2
            if span >= c:
                break
            pw = _bmm3(pw_parts, pw_parts)
        e_gc = jnp.exp(gc_col)
        t_bf = t_inv.astype(BF16)
        u = jnp.einsum('gij,gjd->gid', t_bf, (v * beta).astype(BF16), preferred_element_type=F32)
        w = jnp.einsum('gij,gjd->gid', t_bf, (kb * e_gc).astype(BF16), preferred_element_type=F32)
        qk = jnp.einsum('gid,gjd->gij', q.astype(BF16), k_bf, preferred_element_type=F32)
        g_last = gc_col[:, c - 1:c, :]
        u_s[rows, :] = u.reshape(GDN_GROUP * c, GDN_DK)
        w_s[rows, :] = w.astype(BF16).reshape(GDN_GROUP * c, GDN_DK)
        qd_s[rows, :] = (q * e_gc).astype(BF16).reshape(GDN_GROUP * c, GDN_DK)
        kd_s[rows, :] = (k * jnp.exp(g_last - gc_col)).astype(BF16).reshape(GDN_GROUP * c, GDN_DK)
        attn_s[rows, :] = (qk * decay).astype(BF16).reshape(GDN_GROUP * c, c)
        egl_s[pl.ds(grp * GDN_GROUP, GDN_GROUP)] = jnp.broadcast_to(jnp.exp(g_last), (GDN_GROUP, 1, GDN_DK))
        return carry

    lax.fori_loop(0, n_chunks // GDN_GROUP, local, 0)

    def scan(ci, state):
        rows = pl.ds(pl.multiple_of(ci * c, c), c)
        s_bf = state.astype(BF16)
        v_new = u_s[rows, :] - jnp.dot(w_s[rows, :], s_bf, preferred_element_type=F32)
        v_bf = v_new.astype(BF16)
        o = (jnp.dot(qd_s[rows, :], s_bf, preferred_element_type=F32)
             + jnp.dot(attn_s[rows, :], v_bf, preferred_element_type=F32))
        state = state * egl_s[ci] + lax.dot_general(kd_s[rows, :], v_bf, (((0,), (0,)), ((), ())),
                                                    preferred_element_type=F32)
        o = o * lax.rsqrt(jnp.mean(o * o, axis=-1, keepdims=True) + EPS) * gain_ref[...]
        o_ref[0, rows, :] = (o * _silu(z_ref[0, rows, :])).astype(o_ref.dtype)
        return state

    lax.fori_loop(0, n_chunks, scan, jnp.zeros((GDN_DK, GDN_DK), F32))


def _gdn(qkv, p3, ab3, a_rows, a_log, dt_bias, o_gain):
    b, seq, _ = qkv.shape
    hblk = HALF // GDN_DK
    smem = pl.BlockSpec(memory_space=pltpu.SMEM)

    def head(col_block0):
        return pl.BlockSpec((1, seq, GDN_DK), lambda bi, hi: (bi, 0, col_block0 + hi))

    return pl.pallas_call(
        functools.partial(_gdn_kernel, seq=seq),
        grid=(b, GDN_HEADS),
        in_specs=[
            smem, smem,
            head(0), head(hblk), head(2 * hblk), head(4 * hblk),
            pl.BlockSpec((1, seq, LANES), lambda bi, hi: (bi, 0, 0)),
            pl.BlockSpec((1, 1, seq // CHUNK, 1, CHUNK), lambda bi, hi: (bi, hi, 0, 0, 0)),
            pl.BlockSpec((1, GDN_DK), lambda bi, hi: (0, 0)),
        ],
        out_specs=pl.BlockSpec((1, seq, GDN_DK), lambda bi, hi: (bi, 0, hi)),
        out_shape=jax.ShapeDtypeStruct((b, seq, HALF), BF16),
        scratch_shapes=[
            pltpu.VMEM((seq, GDN_DK), F32), pltpu.VMEM((seq, GDN_DK), BF16),
            pltpu.VMEM((seq, GDN_DK), BF16), pltpu.VMEM((seq, GDN_DK), BF16),
            pltpu.VMEM((seq, CHUNK), BF16), pltpu.VMEM((seq // CHUNK, 1, GDN_DK), F32),
        ],
        compiler_params=_params("parallel", "arbitrary"),
        name="gdn",
    )(a_log, dt_bias, qkv, qkv, qkv, p3, ab3, a_rows, o_gain.reshape(1, GDN_DK))


def _top_rows(s, k, payload=None):
    n = s.shape[0]
    row = lax.broadcasted_iota(jnp.int32, s.shape, 0)
    vals, picks = [], []
    for _ in range(k):
        m = jnp.max(s, axis=0, keepdims=True)
        idx = jnp.min(jnp.where(s == m, row, n), axis=0, keepdims=True)
        hit = row == idx
        vals.append(m)
        picks.append(idx if payload is None else jnp.max(jnp.where(hit, payload, -1), axis=0, keepdims=True))
        s = jnp.where(hit, -jnp.inf, s)
    return jnp.concatenate(vals, axis=0), jnp.concatenate(picks, axis=0)


def _peer_select_kernel(q_ref, keys_ref, e_ref, g_ref):
    half_q = PEER_KEYS
    for h in range(PEER_HEADS):
        sv, si = [], []
        for p in range(2):
            qc = q_ref[:, (2 * h + p) * half_q:(2 * h + p + 1) * half_q]
            scores_t = lax.dot_general(keys_ref[p], qc, (((1,), (1,)), ((), ())),
                                       precision=HI, preferred_element_type=F32)
            v, i = _top_rows(scores_t, PEER_TOPK)
            sv.append(v)
            si.append(i)
        cand = jnp.concatenate([sv[0][i:i + 1] + sv[1] for i in range(PEER_TOPK)], axis=0)
        cand_e = jnp.concatenate([si[0][i:i + 1] * PEER_KEYS + si[1] for i in range(PEER_TOPK)], axis=0)
        cv, ce = _top_rows(cand, PEER_TOPK, payload=cand_e)
        ex = jnp.exp(cv - cv[0:1])
        e_ref[h * PEER_TOPK:(h + 1) * PEER_TOPK, :] = ce
        g_ref[h * PEER_TOPK:(h + 1) * PEER_TOPK, :] = ex / jnp.sum(ex, axis=0, keepdims=True)


def _peer_select(q, keys, tokens=256):
    t = q.shape[0]
    return pl.pallas_call(
        _peer_select_kernel,
        grid=(t // tokens,),
        in_specs=[pl.BlockSpec((tokens, q.shape[1]), lambda i: (i, 0)),
                  pl.BlockSpec(keys.shape, lambda i: (0, 0, 0))],
        out_specs=[pl.BlockSpec((PEER_PICKS, tokens), lambda i: (0, i)),
                   pl.BlockSpec((PEER_PICKS, tokens), lambda i: (0, i))],
        out_shape=[jax.ShapeDtypeStruct((PEER_PICKS, t), jnp.int32),
                   jax.ShapeDtypeStruct((PEER_PICKS, t), F32)],
        compiler_params=_params("parallel"),
        name="peer_select",
    )(q, keys)


PEER_TOKENS = 64
PEER_SLOTS = 4
PEER_AHEAD = PEER_SLOTS - 1
PEER_ROW_GROUPS = PEER_PICKS // SUBLANES


def _peer_gather_kernel(e_ref, en_ref, x_ref, gt_ref, r_ref, u_all, v_all, o_ref, ubuf, vbuf, usem, vsem,
                        *, layer, n_steps):
    step = pl.program_id(0)
    u_hbm = u_all.at[layer]
    v_hbm = v_all.at[layer]

    def start_rows(tab, buf, sem, idx_ref, tok, slot, group):
        for j in range(group * SUBLANES, (group + 1) * SUBLANES):
            e = idx_ref[tok, j]
            pltpu.make_async_copy(tab.at[pl.ds(e, 1), :], buf.at[slot, pl.ds(j, 1), :], sem.at[slot]).start(priority=j % 2)

    def start_token(idx_ref, tok, slot):
        for g in range(PEER_ROW_GROUPS):
            start_rows(u_hbm, ubuf, usem, idx_ref, tok, slot, g)
            start_rows(v_hbm, vbuf, vsem, idx_ref, tok, slot, g)

    def wait_token(slot):
        pltpu.make_async_copy(u_hbm.at[pl.ds(0, PEER_PICKS), :], ubuf.at[slot], usem.at[slot]).wait()
        pltpu.make_async_copy(v_hbm.at[pl.ds(0, PEER_PICKS), :], vbuf.at[slot], vsem.at[slot]).wait()

    lane = lax.broadcasted_iota(jnp.int32, (PEER_PICKS, PEER_TOKENS), 1)

    def consume(tok, start_u_group, start_v_group):
        slot = tok % PEER_SLOTS
        wait_token(slot)
        x = x_ref[pl.ds(tok, 1), :]
        gate = jnp.sum(jnp.where(lane == tok, gt_ref[0], 0.0), axis=-1, keepdims=True)
        hid = []
        for g in range(PEER_ROW_GROUPS):
            rows = pl.ds(g * SUBLANES, SUBLANES)
            hid.append(jnp.sum(ubuf[slot, rows, :] * x, axis=-1, keepdims=True))
            start_u_group(g)
        act = _gelu(jnp.concatenate(hid, axis=0)) * gate
        acc = None
        for g in range(PEER_ROW_GROUPS):
            rows = pl.ds(g * SUBLANES, SUBLANES)
            part = vbuf[slot, rows, :] * act[g * SUBLANES:(g + 1) * SUBLANES]
            acc = part if acc is None else acc + part
            start_v_group(g)
        y = jnp.sum(acc, axis=0, keepdims=True)
        o_ref[pl.ds(tok, 1), :] = r_ref[pl.ds(tok, 1), :] + y

    @pl.when(step == 0)
    def _():
        for s in range(PEER_AHEAD):
            start_token(e_ref, s, s)

    def steady(tok, carry):
        nxt = tok + PEER_AHEAD
        nslot = nxt % PEER_SLOTS
        consume(tok,
                lambda g: start_rows(u_hbm, ubuf, usem, e_ref, nxt, nslot, g),
                lambda g: start_rows(v_hbm, vbuf, vsem, e_ref, nxt, nslot, g))
        return carry

    lax.fori_loop(0, PEER_TOKENS - PEER_AHEAD, steady, 0)

    for tok in range(PEER_TOKENS - PEER_AHEAD, PEER_TOKENS):
        @pl.when(step < n_steps - 1)
        def _():
            start_token(en_ref, tok + PEER_AHEAD - PEER_TOKENS, (tok + PEER_AHEAD) % PEER_SLOTS)

        consume(tok, lambda g: None, lambda g: None)


def _peer_gather(experts, xn, gates_t, resid, u_tabs, v_tabs, layer):
    t, d = xn.shape
    blk = PEER_TOKENS
    n_steps = t // blk
    return pl.pallas_call(
        functools.partial(_peer_gather_kernel, layer=layer, n_steps=n_steps),
        grid=(n_steps,),
        in_specs=[
            pl.BlockSpec((blk, PEER_PICKS), lambda i: (i, 0), memory_space=pltpu.SMEM),
            pl.BlockSpec((blk, PEER_PICKS), lambda i: (jnp.minimum(i + 1, n_steps - 1), 0), memory_space=pltpu.SMEM),
            pl.BlockSpec((blk, d), lambda i: (i, 0)),
            pl.BlockSpec((1, PEER_PICKS, blk), lambda i: (i, 0, 0)),
            pl.BlockSpec((blk, d), lambda i: (i, 0)),
            pl.BlockSpec(memory_space=pl.ANY),
            pl.BlockSpec(memory_space=pl.ANY),
        ],
        out_specs=pl.BlockSpec((blk, d), lambda i: (i, 0)),
        out_shape=jax.ShapeDtypeStruct((t, d), F32),
        scratch_shapes=[
            pltpu.VMEM((PEER_SLOTS, PEER_PICKS, d), F32),
            pltpu.VMEM((PEER_SLOTS, PEER_PICKS, d), F32),
            pltpu.SemaphoreType.DMA((PEER_SLOTS,)),
            pltpu.SemaphoreType.DMA((PEER_SLOTS,)),
        ],
        compiler_params=_params("arbitrary"),
        name="peer_gather",
    )(experts, experts, xn, gates_t, resid, u_tabs, v_tabs)


def _peer_ffn(h, norm_g, w_q, keys, u_tabs, v_tabs, layer):
    t = h.shape[0]
    xn, xn_bf = _rmsnorm(h, norm_g, (F32, BF16))
    q = _matmul(xn_bf, w_q, w_q.shape[1])
    experts_t, gates_t = _peer_select(q, keys)
    experts = experts_t.T
    gates_blk = gates_t.reshape(PEER_PICKS, t // PEER_TOKENS, PEER_TOKENS).transpose(1, 0, 2)
    return _peer_gather(experts, xn, gates_blk, h, u_tabs, v_tabs, layer)


def kernel(x, ab_norm, ab_w_in, ab_conv, ab_ln_g, ab_ln_b, ab_w_s, ab_b_s, ab_w_out, cd_norm, cd_w_in, cd_pool_w,
           cd_pool_scale, cd_conv_qkv, cd_a_log, cd_dt_bias, cd_o_gain, cd_w_out, ffn_norm, peer_wq, peer_keys,
           peer_u, peer_v, final_norm):
    b, seq, d = x.shape
    t = b * seq
    h = x.reshape(t, d)

    (xn,) = _rmsnorm(h, ab_norm[0], (BF16,))
    p = _matmul(xn, ab_w_in[0], 5 * HALF)
    y = _mixer_ab(p, seq, ab_conv[0], ab_ln_g[0], ab_ln_b[0], ab_w_s[0], ab_b_s[0])
    h = _matmul(y, ab_w_out[0], d, residual=h)
    h = _peer_ffn(h, ffn_norm[0], peer_wq[0], peer_keys[0], peer_u, peer_v, 0)

    (xn,) = _rmsnorm(h, cd_norm[0], (BF16,))
    p = _matmul(xn, cd_w_in[0], 5 * HALF)
    w_gates = jnp.pad(cd_w_in[0][:, 5 * HALF:], ((0, 0), (0, LANES - 2 * GDN_HEADS)))
    ab = _matmul(xn, w_gates, LANES)
    yc = _mixer_c(p, seq, cd_pool_w[0], cd_pool_scale[0])
    qkv = _gdn_prep(p, seq, cd_conv_qkv[0])
    ab3 = ab.reshape(b, seq, LANES)
    a_rows = ab3[:, :, :GDN_HEADS].transpose(0, 2, 1).reshape(b, GDN_HEADS, seq // CHUNK, 1, CHUNK)
    yd = _gdn(qkv.reshape(b, seq, 3 * HALF), p.reshape(b, seq, 5 * HALF), ab3, a_rows,
              cd_a_log[0], cd_dt_bias[0], cd_o_gain[0])
    y = jnp.concatenate([yc, yd.reshape(t, HALF)], axis=1)
    h = _matmul(y, cd_w_out[0], d, residual=h)
    h = _peer_ffn(h, ffn_norm[1], peer_wq[1], peer_keys[1], peer_u, peer_v, 1)

    (out,) = _rmsnorm(h, final_norm, (F32,))
    return out.reshape(b, seq, d)
```

```python
import functools
import math

import jax
import jax.numpy as jnp
from jax import lax
from jax.experimental import pallas as pl
from jax.experimental.pallas import tpu as pltpu

F32 = jnp.float32
BF16 = jnp.bfloat16
EPS = 1e-6

D_MODEL = 4096
HALF = D_MODEL // 2
CHUNK = 64
SGU_BLOCK = 128
SGU_GROUPS = 8
SGU_GROUP_WIDTH = HALF // SGU_GROUPS
POOL_WINDOWS = (2, 4, 8, 16)
POOL_GROUP = HALF // len(POOL_WINDOWS)
GDN_HEADS = 16
GDN_DK = HALF // GDN_HEADS
GDN_CONV = 4
PEER_HEADS = 8
PEER_KEYS = 128
PEER_TOPK = 16
PEER_PICKS = PEER_HEADS * PEER_TOPK
LANES = 128
SUBLANES = 8
VMEM_LIMIT_BYTES = 56 * 1024 * 1024
HI = lax.Precision.HIGHEST


def _params(*semantics):
    return pltpu.CompilerParams(dimension_semantics=semantics, vmem_limit_bytes=VMEM_LIMIT_BYTES)


def _gelu(x):
    return 0.5 * x * (1.0 + jnp.tanh(math.sqrt(2.0 / math.pi) * (x + 0.044715 * (x * x * x))))


def _silu(x):
    return x * (1.0 / (1.0 + jnp.exp(-x)))


def _softplus(x):
    return jnp.maximum(x, 0.0) + jnp.log(1.0 + jnp.exp(-jnp.abs(x)))


def _rmsnorm_kernel(x_ref, g_ref, *o_refs):
    x = x_ref[...]
    y = x * lax.rsqrt(jnp.mean(x * x, axis=-1, keepdims=True) + EPS) * g_ref[...]
    for o_ref in o_refs:
        o_ref[...] = y.astype(o_ref.dtype)


def _rmsnorm(x, g, out_dtypes, rows=256):
    t, d = x.shape
    outs = pl.pallas_call(
        _rmsnorm_kernel,
        grid=(t // rows,),
        in_specs=[pl.BlockSpec((rows, d), lambda i: (i, 0)), pl.BlockSpec((1, d), lambda i: (0, 0))],
        out_specs=[pl.BlockSpec((rows, d), lambda i: (i, 0)) for _ in out_dtypes],
        out_shape=[jax.ShapeDtypeStruct((t, d), dt) for dt in out_dtypes],
        compiler_params=_params("parallel"),
        name="rmsnorm",
    )(x, g.reshape(1, d))
    return outs


def _matmul_kernel(*refs, n_x, has_residual):
    x_refs, w_ref, o_ref = refs[:n_x], refs[n_x], refs[-1]
    acc, k0 = None, 0
    for x_ref in x_refs:
        k1 = k0 + x_ref.shape[1]
        part = jnp.dot(x_ref[...], w_ref[k0:k1, :].astype(BF16), preferred_element_type=F32)
        acc, k0 = (part if acc is None else acc + part), k1
    if has_residual:
        acc = acc + refs[n_x + 1][...]
    o_ref[...] = acc


def _matmul(xs, w, n_out, residual=None, tm=1024, tn=512):
    xs = xs if isinstance(xs, (tuple, list)) else (xs,)
    m = xs[0].shape[0]
    k = sum(x.shape[1] for x in xs)
    tm = min(tm, m)
    tn = min(tn, n_out)
    in_specs = [pl.BlockSpec((tm, x.shape[1]), lambda i, j: (i, 0)) for x in xs]
    in_specs.append(pl.BlockSpec((k, tn), lambda i, j: (0, j)))
    args = [*xs, w]
    if residual is not None:
        in_specs.append(pl.BlockSpec((tm, tn), lambda i, j: (i, j)))
        args.append(residual)
    return pl.pallas_call(
        functools.partial(_matmul_kernel, n_x=len(xs), has_residual=residual is not None),
        grid=(m // tm, n_out // tn),
        in_specs=in_specs,
        out_specs=pl.BlockSpec((tm, tn), lambda i, j: (i, j)),
        out_shape=jax.ShapeDtypeStruct((m, n_out), F32),
        compiler_params=_params("parallel", "arbitrary"),
        name="matmul",
    )(*args)


def _mixer_ab_kernel(p_ref, hc_ref, hh_ref, cw_ref, lng_ref, lnb_ref, ws_ref, bs_ref, o_ref, *, blocks_per_seq):
    first = (pl.program_id(0) % blocks_per_seq) == 0
    gate_b = p_ref[:, 0:HALF]
    z = p_ref[:, HALF:2 * HALF] * p_ref[:, 2 * HALF:3 * HALF]
    z_halo = jnp.where(first, 0.0, hc_ref[...] * hh_ref[...])
    zz = jnp.concatenate([z_halo, z], axis=0)
    z1 = pltpu.roll(zz, 1, axis=0)[SUBLANES:]
    z2 = pltpu.roll(zz, 2, axis=0)[SUBLANES:]
    conv = cw_ref[0:1, :] * z2 + cw_ref[1:2, :] * z1 + cw_ref[2:3, :] * z
    o_ref[:, 0:HALF] = (gate_b * conv).astype(o_ref.dtype)

    u = _gelu(p_ref[:, 3 * HALF:4 * HALF])
    v = _gelu(p_ref[:, 4 * HALF:5 * HALF])
    mu = jnp.mean(v, axis=-1, keepdims=True)
    vc = v - mu
    var = jnp.mean(vc * vc, axis=-1, keepdims=True)
    vn = vc * lax.rsqrt(var + EPS) * lng_ref[...] + lnb_ref[...]
    row = lax.broadcasted_iota(jnp.int32, (SGU_BLOCK, SGU_BLOCK), 0)
    col = lax.broadcasted_iota(jnp.int32, (SGU_BLOCK, SGU_BLOCK), 1)
    for g in range(SGU_GROUPS):
        cols = slice(g * SGU_GROUP_WIDTH, (g + 1) * SGU_GROUP_WIDTH)
        w = jnp.where(row >= col, ws_ref[g], 0.0).astype(BF16)
        mixed = jnp.dot(w, vn[:, cols].astype(BF16), preferred_element_type=F32) + bs_ref[:, g:g + 1]
        o_ref[:, HALF + g * SGU_GROUP_WIDTH:HALF + (g + 1) * SGU_GROUP_WIDTH] = (u[:, cols] * mixed).astype(o_ref.dtype)


def _mixer_ab(p, seq, conv_w, ln_g, ln_b, w_s, b_s):
    t = p.shape[0]
    rows = SGU_BLOCK
    halo_blocks = rows // SUBLANES

    def halo_map(col_block):
        return lambda i: (jnp.maximum(i * halo_blocks - 1, 0), col_block)

    return pl.pallas_call(
        functools.partial(_mixer_ab_kernel, blocks_per_seq=seq // rows),
        grid=(t // rows,),
        in_specs=[
            pl.BlockSpec((rows, 5 * HALF), lambda i: (i, 0)),
            pl.BlockSpec((SUBLANES, HALF), halo_map(1)),
            pl.BlockSpec((SUBLANES, HALF), halo_map(2)),
            pl.BlockSpec((3, HALF), lambda i: (0, 0)),
            pl.BlockSpec((1, HALF), lambda i: (0, 0)),
            pl.BlockSpec((1, HALF), lambda i: (0, 0)),
            pl.BlockSpec((SGU_GROUPS, SGU_BLOCK, SGU_BLOCK), lambda i: (0, 0, 0)),
            pl.BlockSpec((SGU_BLOCK, SGU_GROUPS), lambda i: (0, 0)),
        ],
        out_specs=pl.BlockSpec((rows, D_MODEL), lambda i: (i, 0)),
        out_shape=jax.ShapeDtypeStruct((t, D_MODEL), BF16),
        compiler_params=_params("parallel"),
        name="mixer_ab",
    )(p, p, p, conv_w.T, ln_g.reshape(1, HALF), ln_b.reshape(1, HALF), w_s, b_s.T)


def _mixer_c_kernel(p_ref, halo_ref, w_ref, scale_ref, o_ref, *, rows, blocks_per_seq):
    blk = pl.program_id(0) % blocks_per_seq
    halo_rows = halo_ref.shape[0]
    x = p_ref[...]
    halo = jnp.where(blk == 0, 0.0, halo_ref[...])
    s = jnp.concatenate([halo, x], axis=0)
    pos = blk * rows + lax.broadcasted_iota(jnp.int32, (rows, 1), 0)
    shift = 1
    for g, win in enumerate(POOL_WINDOWS):
        s = s[:, (POOL_GROUP if g else 0):]
        while shift < win:
            s = s + pltpu.roll(s, shift, axis=0)
            shift *= 2
        cnt = jnp.minimum(pos + 1, win).astype(F32)
        xg = x[:, g * POOL_GROUP:(g + 1) * POOL_GROUP]
        pooled = s[halo_rows:, :POOL_GROUP] / cnt - xg
        y = jnp.dot(pooled.astype(BF16), w_ref[g].astype(BF16), preferred_element_type=F32)
        o_ref[:, g * POOL_GROUP:(g + 1) * POOL_GROUP] = (y * scale_ref[:, g * POOL_GROUP:(g + 1) * POOL_GROUP]).astype(o_ref.dtype)


def _mixer_c(p, seq, pool_w, pool_scale, rows=256):
    t = p.shape[0]
    halo_rows = max(POOL_WINDOWS)
    return pl.pallas_call(
        functools.partial(_mixer_c_kernel, rows=rows, blocks_per_seq=seq // rows),
        grid=(t // rows,),
        in_specs=[
            pl.BlockSpec((rows, HALF), lambda i: (i, 0)),
            pl.BlockSpec((halo_rows, HALF), lambda i: (jnp.maximum(i * (rows // halo_rows) - 1, 0), 0)),
            pl.BlockSpec((len(POOL_WINDOWS), POOL_GROUP, POOL_GROUP), lambda i: (0, 0, 0)),
            pl.BlockSpec((1, HALF), lambda i: (0, 0)),
        ],
        out_specs=pl.BlockSpec((rows, HALF), lambda i: (i, 0)),
        out_shape=jax.ShapeDtypeStruct((t, HALF), BF16),
        compiler_params=_params("parallel"),
        name="mixer_c",
    )(p, p, pool_w, pool_scale.reshape(1, HALF))


def _gdn_prep_kernel(x_ref, halo_ref, cw_ref, o_ref, *, blocks_per_seq):
    first = (pl.program_id(0) % blocks_per_seq) == 0
    section = pl.program_id(1)
    x = x_ref[...]
    halo = jnp.where(first, 0.0, halo_ref[...])
    xx = jnp.concatenate([halo, x], axis=0)
    acc = cw_ref[GDN_CONV - 1:GDN_CONV, :] * x
    for lag in range(1, GDN_CONV):
        acc = acc + cw_ref[GDN_CONV - 1 - lag:GDN_CONV - lag, :] * pltpu.roll(xx, lag, axis=0)[SUBLANES:]
    act = _silu(acc)
    q_scale = jnp.where(section == 0, GDN_DK ** -0.5, 1.0)
    for h in range(GDN_HEADS):
        a = act[:, h * GDN_DK:(h + 1) * GDN_DK]
        normed = a * (lax.rsqrt(jnp.sum(a * a, axis=-1, keepdims=True) + EPS) * q_scale)
        o_ref[:, h * GDN_DK:(h + 1) * GDN_DK] = jnp.where(section < 2, normed, a)


def _gdn_prep(p, seq, conv_w, rows=256):
    t = p.shape[0]
    return pl.pallas_call(
        functools.partial(_gdn_prep_kernel, blocks_per_seq=seq // rows),
        grid=(t // rows, 3),
        in_specs=[
            pl.BlockSpec((rows, HALF), lambda i, j: (i, 1 + j)),
            pl.BlockSpec((SUBLANES, HALF), lambda i, j: (jnp.maximum(i * (rows // SUBLANES) - 1, 0), 1 + j)),
            pl.BlockSpec((GDN_CONV, HALF), lambda i, j: (0, j)),
        ],
        out_specs=pl.BlockSpec((rows, HALF), lambda i, j: (i, j)),
        out_shape=jax.ShapeDtypeStruct((t, 3 * HALF), F32),
        compiler_params=_params("parallel", "arbitrary"),
        name="gdn_prep",
    )(p, p, conv_w.T)


GDN_GROUP = 8


def _split_bf16(a):
    hi = a.astype(BF16)
    return hi, (a - hi.astype(F32)).astype(BF16)


def _bmm3(a_parts, b_parts):
    mm = functools.partial(jnp.einsum, 'gij,gjk->gik', preferred_element_type=F32)
    return mm(a_parts[0], b_parts[0]) + (mm(a_parts[0], b_parts[1]) + mm(a_parts[1], b_parts[0]))


def _gdn_kernel(alog_ref, dtb_ref, q_ref, k_ref, v_ref, z_ref, ab_ref, arow_ref, gain_ref, o_ref,
                u_s, w_s, qd_s, kd_s, attn_s, egl_s, *, seq):
    h = pl.program_id(1)
    n_chunks = seq // CHUNK
    c = CHUNK
    neg_a = -jnp.exp(jnp.full((1, 1), alog_ref[h], F32))
    dt_bias = dtb_ref[h]
    ii = lax.broadcasted_iota(jnp.int32, (c, c), 0)
    jj = lax.broadcasted_iota(jnp.int32, (c, c), 1)
    lower = (ii >= jj)[None]
    strict = (ii > jj)[None]
    eye = (ii == jj).astype(F32)[None]
    lane = lax.broadcasted_iota(jnp.int32, (1, 1, LANES), 2)

    def local(grp, carry):
        c0 = pl.multiple_of(grp * (GDN_GROUP * c), GDN_GROUP * c)
        rows = pl.ds(c0, GDN_GROUP * c)
        q = q_ref[0, rows, :].reshape(GDN_GROUP, c, GDN_DK)
        k = k_ref[0, rows, :].reshape(GDN_GROUP, c, GDN_DK)
        v = v_ref[0, rows, :].reshape(GDN_GROUP, c, GDN_DK)
        ab = ab_ref[0, rows, :].reshape(GDN_GROUP, c, LANES)
        a_col = jnp.sum(jnp.where(lane == h, ab, 0.0), axis=-1, keepdims=True)
        b_col = jnp.sum(jnp.where(lane == h + GDN_HEADS, ab, 0.0), axis=-1, keepdims=True)
        a_row = arow_ref[0, 0, pl.ds(grp * GDN_GROUP, GDN_GROUP), :, :]
        g_col = neg_a * _softplus(a_col + dt_bias)
        g_row = neg_a * _softplus(a_row + dt_bias)
        beta = 1.0 / (1.0 + jnp.exp(-b_col))
        gc_col = jnp.sum(jnp.where(lower, g_row, 0.0), axis=-1, keepdims=True)
        gc_row = jnp.sum(jnp.where(strict, 0.0, g_col), axis=1, keepdims=True)
        decay = jnp.where(lower, jnp.exp(jnp.where(lower, gc_col - gc_row, 0.0)), 0.0)
        kb = k * beta
        k_bf = k.astype(BF16)
        kk = jnp.einsum('gid,gjd->gij', kb.astype(BF16), k_bf, preferred_element_type=F32)
        n = jnp.where(strict, kk * decay, 0.0)
        t_inv = eye - n
        pw_parts = _split_bf16(n)
        pw = _bmm3(pw_parts, pw_parts)
        span = 2
        while True:
            pw_parts = _split_bf16(pw)
            t_inv = t_inv + _bmm3(_split_bf16(t_inv), pw_parts)
            span *= 2
            if span >= c:
                break
            pw = _bmm3(pw_parts, pw_parts)
        e_gc = jnp.exp(gc_col)
        t_bf = t_inv.astype(BF16)
        u = jnp.einsum('gij,gjd->gid', t_bf, (v * beta).astype(BF16), preferred_element_type=F32)
        w = jnp.einsum('gij,gjd->gid', t_bf, (kb * e_gc).astype(BF16), preferred_element_type=F32)
        qk = jnp.einsum('gid,gjd->gij', q.astype(BF16), k_bf, preferred_element_type=F32)
        g_last = gc_col[:, c - 1:c, :]
        u_s[rows, :] = u.reshape(GDN_GROUP * c, GDN_DK)
        w_s[rows, :] = w.astype(BF16).reshape(GDN_GROUP * c, GDN_DK)
        qd_s[rows, :] = (q * e_gc).astype(BF16).reshape(GDN_GROUP * c, GDN_DK)
        kd_s[rows, :] = (k * jnp.exp(g_last - gc_col)).astype(BF16).reshape(GDN_GROUP * c, GDN_DK)
        attn_s[rows, :] = (qk * decay).astype(BF16).reshape(GDN_GROUP * c, c)
        egl_s[pl.ds(grp * GDN_GROUP, GDN_GROUP)] = jnp.broadcast_to(jnp.exp(g_last), (GDN_GROUP, 1, GDN_DK))
        return carry

    lax.fori_loop(0, n_chunks // GDN_GROUP, local, 0)

    def scan(ci, state):
        rows = pl.ds(pl.multiple_of(ci * c, c), c)
        s_bf = state.astype(BF16)
        v_new = u_s[rows, :] - jnp.dot(w_s[rows, :], s_bf, preferred_element_type=F32)
        v_bf = v_new.astype(BF16)
        o = (jnp.dot(qd_s[rows, :], s_bf, preferred_element_type=F32)
             + jnp.dot(attn_s[rows, :], v_bf, preferred_element_type=F32))
        state = state * egl_s[ci] + lax.dot_general(kd_s[rows, :], v_bf, (((0,), (0,)), ((), ())),
                                                    preferred_element_type=F32)
        o = o * lax.rsqrt(jnp.mean(o * o, axis=-1, keepdims=True) + EPS) * gain_ref[...]
        o_ref[0, rows, :] = (o * _silu(z_ref[0, rows, :])).astype(o_ref.dtype)
        return state

    lax.fori_loop(0, n_chunks, scan, jnp.zeros((GDN_DK, GDN_DK), F32))


def _gdn(qkv, p3, ab3, a_rows, a_log, dt_bias, o_gain):
    b, seq, _ = qkv.shape
    assert seq % (CHUNK * GDN_GROUP) == 0, "sequence must split into whole groups of chunks"
    hblk = HALF // GDN_DK
    smem = pl.BlockSpec(memory_space=pltpu.SMEM)

    def head(col_block0):
        return pl.BlockSpec((1, seq, GDN_DK), lambda bi, hi: (bi, 0, col_block0 + hi))

    return pl.pallas_call(
        functools.partial(_gdn_kernel, seq=seq),
        grid=(b, GDN_HEADS),
        in_specs=[
            smem, smem,
            head(0), head(hblk), head(2 * hblk), head(4 * hblk),
            pl.BlockSpec((1, seq, LANES), lambda bi, hi: (bi, 0, 0)),
            pl.BlockSpec((1, 1, seq // CHUNK, 1, CHUNK), lambda bi, hi: (bi, hi, 0, 0, 0)),
            pl.BlockSpec((1, GDN_DK), lambda bi, hi: (0, 0)),
        ],
        out_specs=pl.BlockSpec((1, seq, GDN_DK), lambda bi, hi: (bi, 0, hi)),
        out_shape=jax.ShapeDtypeStruct((b, seq, HALF), BF16),
        scratch_shapes=[
            pltpu.VMEM((seq, GDN_DK), F32), pltpu.VMEM((seq, GDN_DK), BF16),
            pltpu.VMEM((seq, GDN_DK), BF16), pltpu.VMEM((seq, GDN_DK), BF16),
            pltpu.VMEM((seq, CHUNK), BF16), pltpu.VMEM((seq // CHUNK, 1, GDN_DK), F32),
        ],
        compiler_params=_params("parallel", "arbitrary"),
        name="gdn",
    )(a_log, dt_bias, qkv, qkv, qkv, p3, ab3, a_rows, o_gain.reshape(1, GDN_DK))


def _top_rows(s, k, payload=None):
    n = s.shape[0]
    row = lax.broadcasted_iota(jnp.int32, s.shape, 0)
    vals, picks = [], []
    for _ in range(k):
        m = jnp.max(s, axis=0, keepdims=True)
        idx = jnp.min(jnp.where(s == m, row, n), axis=0, keepdims=True)
        hit = row == idx
        vals.append(m)
        picks.append(idx if payload is None else jnp.max(jnp.where(hit, payload, -1), axis=0, keepdims=True))
        s = jnp.where(hit, -jnp.inf, s)
    return jnp.concatenate(vals, axis=0), jnp.concatenate(picks, axis=0)


def _peer_select_kernel(q_ref, keys_ref, e_ref, g_ref):
    half_q = PEER_KEYS
    for h in range(PEER_HEADS):
        sv, si = [], []
        for p in range(2):
            qc = q_ref[:, (2 * h + p) * half_q:(2 * h + p + 1) * half_q]
            scores_t = lax.dot_general(keys_ref[p], qc, (((1,), (1,)), ((), ())),
                                       precision=HI, preferred_element_type=F32)
            v, i = _top_rows(scores_t, PEER_TOPK)
            sv.append(v)
            si.append(i)
        cand, cand_e = [], []
        sub_row = lax.broadcasted_iota(jnp.int32, (SUBLANES, 1), 0)
        for i in range(SUBLANES):
            width = PEER_TOPK if i == 0 else SUBLANES
            c = sv[0][i:i + 1] + sv[1][:width]
            if PEER_TOPK // (i + 1) < width:
                c = jnp.where(sub_row < PEER_TOPK // (i + 1), c, -jnp.inf)
            cand.append(c)
            cand_e.append(si[0][i:i + 1] * PEER_KEYS + si[1][:width])
        cand.append(sv[0][SUBLANES:] + sv[1][0:1])
        cand_e.append(si[0][SUBLANES:] * PEER_KEYS + si[1][0:1])
        cv, ce = _top_rows(jnp.concatenate(cand, axis=0), PEER_TOPK, payload=jnp.concatenate(cand_e, axis=0))
        ex = jnp.exp(cv - cv[0:1])
        e_ref[h * PEER_TOPK:(h + 1) * PEER_TOPK, :] = ce
        g_ref[h * PEER_TOPK:(h + 1) * PEER_TOPK, :] = ex / jnp.sum(ex, axis=0, keepdims=True)


def _peer_select(q, keys, tokens=256):
    t = q.shape[0]
    return pl.pallas_call(
        _peer_select_kernel,
        grid=(t // tokens,),
        in_specs=[pl.BlockSpec((tokens, q.shape[1]), lambda i: (i, 0)),
                  pl.BlockSpec(keys.shape, lambda i: (0, 0, 0))],
        out_specs=[pl.BlockSpec((PEER_PICKS, tokens), lambda i: (0, i)),
                   pl.BlockSpec((PEER_PICKS, tokens), lambda i: (0, i))],
        out_shape=[jax.ShapeDtypeStruct((PEER_PICKS, t), jnp.int32),
                   jax.ShapeDtypeStruct((PEER_PICKS, t), F32)],
        compiler_params=_params("parallel"),
        name="peer_select",
    )(q, keys)


PEER_TOKENS = 64
PEER_SLOTS = 4
PEER_AHEAD = PEER_SLOTS - 1
PEER_ROW_GROUPS = PEER_PICKS // SUBLANES
PEER_WORD_TILES = HALF // LANES
U32 = jnp.uint32


def _pack_tables_kernel(u_ref, v_ref, o_ref):
    for t, t_ref in enumerate((u_ref, v_ref)):
        x = t_ref[0]
        lo = lax.bitcast_convert_type(x[:, :HALF].astype(BF16).astype(F32), U32)
        hi = lax.bitcast_convert_type(x[:, HALF:].astype(BF16).astype(F32), U32)
        word = hi | lax.shift_right_logical(lo, jnp.uint32(16))
        for c in range(PEER_WORD_TILES):
            o_ref[0, :, t * PEER_WORD_TILES + c, :] = word[:, c * LANES:(c + 1) * LANES]


def _pack_tables(u_tabs, v_tabs, rows=256):
    layers, experts, d = u_tabs.shape
    table = pl.BlockSpec((1, rows, d), lambda l, i: (l, i, 0))
    return pl.pallas_call(
        _pack_tables_kernel,
        grid=(layers, experts // rows),
        in_specs=[table, table],
        out_specs=pl.BlockSpec((1, rows, 2 * PEER_WORD_TILES, LANES), lambda l, i: (l, i, 0, 0)),
        out_shape=jax.ShapeDtypeStruct((layers, experts, 2 * PEER_WORD_TILES, LANES), U32),
        compiler_params=_params("parallel", "parallel"),
        name="pack_tables",
    )(u_tabs, v_tabs)


def _unpack_words(w):
    return (lax.bitcast_convert_type(lax.shift_left(w, jnp.uint32(16)), F32),
            lax.bitcast_convert_type(w & jnp.uint32(0xFFFF0000), F32))


def _peer_gather_kernel(e_ref, en_ref, x_ref, gt_ref, r_ref, tab, o_ref, *scratch, layer, n_steps):
    buf = scratch[:PEER_SLOTS]
    sem = scratch[PEER_SLOTS]
    step = pl.program_id(0)

    pieces = 2 * PEER_WORD_TILES
    rows_per_piece = PEER_PICKS // pieces

    def start_rows(idx_ref, tok, slot, piece):
        for j in range(piece * rows_per_piece, (piece + 1) * rows_per_piece):
            e = idx_ref[tok, j]
            pltpu.make_async_copy(tab.at[layer, e], buf[slot].at[:, j, :], sem.at[slot]).start(priority=j % 2)

    def start_token(idx_ref, tok, slot):
        for p in range(pieces):
            start_rows(idx_ref, tok, slot, p)

    def wait_token(slot):
        pltpu.make_async_copy(buf[slot], buf[slot], sem.at[slot]).wait()

    lane = lax.broadcasted_iota(jnp.int32, (PEER_PICKS, PEER_TOKENS), 1)

    def consume(tok, slot, start_piece):
        wait_token(slot)
        gate = jnp.sum(jnp.where(lane == tok, gt_ref[0], 0.0), axis=-1, keepdims=True)
        x = x_ref[pl.ds(tok, 1), :]
        acc = jnp.zeros((PEER_PICKS, LANES), F32)
        for c in range(PEER_WORD_TILES):
            lo, hi = _unpack_words(buf[slot][c])
            x_lo = x[:, c * LANES:(c + 1) * LANES]
            x_hi = x[:, HALF + c * LANES:HALF + (c + 1) * LANES]
            acc = acc + (lo * x_lo + hi * x_hi)
            start_piece(c)
        hid = jnp.sum(acc, axis=-1, keepdims=True)
        act = jnp.broadcast_to(_gelu(hid) * gate, (PEER_PICKS, LANES))
        y_lo, y_hi = [], []
        for c in range(PEER_WORD_TILES):
            lo, hi = _unpack_words(buf[slot][PEER_WORD_TILES + c])
            y_lo.append(jnp.sum(lo * act, axis=0, keepdims=True))
            y_hi.append(jnp.sum(hi * act, axis=0, keepdims=True))
            start_piece(PEER_WORD_TILES + c)
        o_ref[pl.ds(tok, 1), :] = r_ref[pl.ds(tok, 1), :] + jnp.concatenate(y_lo + y_hi, axis=1)

    @pl.when(step == 0)
    def _():
        for s in range(PEER_AHEAD):
            start_token(e_ref, s, s)

    def consume_and_prefetch(tok, slot):
        nxt = tok + PEER_AHEAD
        nslot = (slot + PEER_AHEAD) % PEER_SLOTS
        consume(tok, slot, lambda p: start_rows(e_ref, nxt, nslot, p))

    steady_tokens = PEER_TOKENS - PEER_AHEAD

    def steady(rnd, carry):
        for slot in range(PEER_SLOTS):
            consume_and_prefetch(rnd * PEER_SLOTS + slot, slot)
        return carry

    lax.fori_loop(0, steady_tokens // PEER_SLOTS, steady, 0)
    for tok in range(steady_tokens // PEER_SLOTS * PEER_SLOTS, steady_tokens):
        consume_and_prefetch(tok, tok % PEER_SLOTS)

    for tok in range(steady_tokens, PEER_TOKENS):
        @pl.when(step < n_steps - 1)
        def _():
            start_token(en_ref, tok + PEER_AHEAD - PEER_TOKENS, (tok + PEER_AHEAD) % PEER_SLOTS)

        consume(tok, tok % PEER_SLOTS, lambda p: None)


def _peer_gather(experts, xn, gates_t, resid, records, layer):
    t, d = xn.shape
    blk = PEER_TOKENS
    n_steps = t // blk
    return pl.pallas_call(
        functools.partial(_peer_gather_kernel, layer=layer, n_steps=n_steps),
        grid=(n_steps,),
        in_specs=[
            pl.BlockSpec((blk, PEER_PICKS), lambda i: (i, 0), memory_space=pltpu.SMEM),
            pl.BlockSpec((blk, PEER_PICKS), lambda i: (jnp.minimum(i + 1, n_steps - 1), 0), memory_space=pltpu.SMEM),
            pl.BlockSpec((blk, d), lambda i: (i, 0)),
            pl.BlockSpec((1, PEER_PICKS, blk), lambda i: (i, 0, 0)),
            pl.BlockSpec((blk, d), lambda i: (i, 0)),
            pl.BlockSpec(memory_space=pl.ANY),
        ],
        out_specs=pl.BlockSpec((blk, d), lambda i: (i, 0)),
        out_shape=jax.ShapeDtypeStruct((t, d), F32),
        scratch_shapes=[
            *[pltpu.VMEM((2 * PEER_WORD_TILES, PEER_PICKS, LANES), U32) for _ in range(PEER_SLOTS)],
            pltpu.SemaphoreType.DMA((PEER_SLOTS,)),
        ],
        compiler_params=_params("arbitrary"),
        name="peer_gather",
    )(experts, experts, xn, gates_t, resid, records)


def _peer_ffn(h, norm_g, w_q, keys, records, layer):
    t = h.shape[0]
    xn, xn_bf = _rmsnorm(h, norm_g, (F32, BF16))
    q = _matmul(xn_bf, w_q, w_q.shape[1])
    experts_t, gates_t = _peer_select(q, keys)
    experts = experts_t.T
    gates_blk = gates_t.reshape(PEER_PICKS, t // PEER_TOKENS, PEER_TOKENS).transpose(1, 0, 2)
    return _peer_gather(experts, xn, gates_blk, h, records, layer)


def kernel(x, ab_norm, ab_w_in, ab_conv, ab_ln_g, ab_ln_b, ab_w_s, ab_b_s, ab_w_out, cd_norm, cd_w_in, cd_pool_w,
           cd_pool_scale, cd_conv_qkv, cd_a_log, cd_dt_bias, cd_o_gain, cd_w_out, ffn_norm, peer_wq, peer_keys,
           peer_u, peer_v, final_norm):
    b, seq, d = x.shape
    t = b * seq
    h = x.reshape(t, d)

    (xn,) = _rmsnorm(h, ab_norm[0], (BF16,))
    p = _matmul(xn, ab_w_in[0], 5 * HALF)
    y = _mixer_ab(p, seq, ab_conv[0], ab_ln_g[0], ab_ln_b[0], ab_w_s[0], ab_b_s[0])
    h = _matmul(y, ab_w_out[0], d, residual=h)
    records = _pack_tables(peer_u, peer_v)
    h = _peer_ffn(h, ffn_norm[0], peer_wq[0], peer_keys[0], records, 0)

    (xn,) = _rmsnorm(h, cd_norm[0], (BF16,))
    p = _matmul(xn, cd_w_in[0], 5 * HALF)
    w_gates = jnp.pad(cd_w_in[0][:, 5 * HALF:], ((0, 0), (0, LANES - 2 * GDN_HEADS)))
    ab = _matmul(xn, w_gates, LANES)
    yc = _mixer_c(p, seq, cd_pool_w[0], cd_pool_scale[0])
    qkv = _gdn_prep(p, seq, cd_conv_qkv[0])
    ab3 = ab.reshape(b, seq, LANES)
    a_rows = ab3[:, :, :GDN_HEADS].transpose(0, 2, 1).reshape(b, GDN_HEADS, seq // CHUNK, 1, CHUNK)
    yd = _gdn(qkv.reshape(b, seq, 3 * HALF), p.reshape(b, seq, 5 * HALF), ab3, a_rows,
              cd_a_log[0], cd_dt_bias[0], cd_o_gain[0])
    h = _matmul((yc, yd.reshape(t, HALF)), cd_w_out[0], d, residual=h)
    h = _peer_ffn(h, ffn_norm[1], peer_wq[1], peer_keys[1], records, 1)
    (out,) = _rmsnorm(h, final_norm, (F32,))
    return out.reshape(b, seq, d)
```
